```python
import math
import jax, jax.numpy as jnp
from jax import lax
import numpy as np

D_MODEL = 1024
BATCH = 2
SEQ = 8192
DEPTH = 2

D_MIX = D_MODEL
N_HEADS_A = 8
HEAD_DIM_A = 64
D_A = N_HEADS_A * HEAD_DIM_A
KV_RANK = 128
N_IDX_HEADS = 8
IDX_DIM = 32
TOPK_MAX = 256
Q_BLOCK = 128
N_POOL_GROUPS = 4
POOL_GROUP_DIM = 64
D_B = N_POOL_GROUPS * POOL_GROUP_DIM
POOL_WINDOWS = (2, 4, 8, 16)
D_C = D_MIX - D_A - D_B
CONV_WIDTH = 31
COL_Q = D_A
COL_KV = COL_Q + KV_RANK
COL_IQ = COL_KV + N_IDX_HEADS * IDX_DIM
COL_IK = COL_IQ + IDX_DIM
COL_IW = COL_IK + N_IDX_HEADS
COL_POOL = COL_IW + D_B
D_IN = COL_POOL + 2 * D_C
SPLIT_POINTS = (COL_Q, COL_KV, COL_IQ, COL_IK, COL_IW, COL_POOL)
N_EXPERTS = 16
N_EXPERT_GROUPS = 4
EXPERTS_PER_GROUP = N_EXPERTS // N_EXPERT_GROUPS
TOP_K = 2
D_EXPERT = 512
MOE_CHUNK = 2048
ALPHA = (2 * DEPTH) ** 0.25
BETA = (8 * DEPTH) ** -0.25
LN_EPS = 1e-5

kernel_name = "hymba_dsa_pool_conformer_groupmoe_deepnorm"


def layer_norm(x, g, b):
    xf = x.astype(jnp.float32)
    mu = jnp.mean(xf, axis=-1, keepdims=True)
    var = jnp.mean(jnp.square(xf - mu), axis=-1, keepdims=True)
    return ((xf - mu) * lax.rsqrt(var + LN_EPS) * g.astype(jnp.float32) + b.astype(jnp.float32)).astype(x.dtype)


def rms_norm(x, g):
    xf = x.astype(jnp.float32)
    ms = jnp.mean(jnp.square(xf), axis=-1, keepdims=True)
    return (xf * lax.rsqrt(ms + LN_EPS) * g.astype(jnp.float32)).astype(x.dtype)


def dsa_attention(q, c_kv, iq, ik, iw, w_uk, w_uv):
    B, S = q.shape[0], q.shape[1]
    topk = min(TOPK_MAX, S // 4)
    nb = S // Q_BLOCK
    q_lat = jnp.einsum('bshd,hrd->bshr', q, w_uk) * (HEAD_DIM_A ** -0.5)
    key_pos = jnp.arange(S)

    def to_blocks(a):
        return a.reshape((B, nb, Q_BLOCK) + a.shape[2:]).swapaxes(0, 1)

    def block_fn(args):
        blk, q_b, iq_b, iw_b = args
        q_pos = blk * Q_BLOCK + jnp.arange(Q_BLOCK)
        logits = jnp.einsum('bqhd,bsd->bqhs', iq_b, ik).astype(jnp.float32)
        idx_score = jnp.einsum('bqhs,bqh->bqs', jax.nn.relu(logits), iw_b.astype(jnp.float32))
        causal = key_pos[None, :] <= q_pos[:, None]
        idx_score = jnp.where(causal[None], idx_score, -jnp.inf)
        _, sel = lax.top_k(idx_score, topk)
        c_sel = jax.vmap(lambda c, i: c[i])(c_kv, sel)
        s = jnp.einsum('bqhr,bqkr->bqhk', q_b, c_sel).astype(jnp.float32)
        valid = sel <= q_pos[None, :, None]
        s = jnp.where(valid[:, :, None, :], s, -jnp.inf)
        p = jax.nn.softmax(s, axis=-1).astype(c_sel.dtype)
        return jnp.einsum('bqhk,bqkr->bqhr', p, c_sel)

    o_lat = lax.map(block_fn, (jnp.arange(nb), to_blocks(q_lat), to_blocks(iq), to_blocks(iw)))
    o_lat = o_lat.swapaxes(0, 1).reshape(B, S, N_HEADS_A, KV_RANK)
    o = jnp.einsum('bshr,hrd->bshd', o_lat, w_uv)
    return o.reshape(B, S, D_A)


def multi_scale_pool(u, w_pool, scale):
    B, S = u.shape[0], u.shape[1]
    ug = u.reshape(B, S, N_POOL_GROUPS, POOL_GROUP_DIM).astype(jnp.float32)
    cs = jnp.cumsum(ug, axis=1)
    t = jnp.arange(1, S + 1, dtype=jnp.float32)
    outs = []
    for gi, w in enumerate(POOL_WINDOWS):
        c = cs[:, :, gi]
        prev = jnp.pad(c, ((0, 0), (w, 0), (0, 0)))[:, :S]
        cnt = jnp.minimum(t, float(w))[None, :, None]
        outs.append((c - prev) / cnt - ug[:, :, gi])
    pooled = jnp.stack(outs, axis=2).astype(u.dtype)
    y = jnp.einsum('bsgc,gcd->bsgd', pooled, w_pool)
    return y.reshape(B, S, D_B) * scale


def conformer_conv(u, conv_w, conv_b, ln_g, ln_b):
    a, gate = jnp.split(u, 2, axis=-1)
    h = a * jax.nn.sigmoid(gate)
    h = lax.conv_general_dilated(h, conv_w, window_strides=(1,), padding=[(CONV_WIDTH - 1, 0)],
                                 dimension_numbers=('NWC', 'WIO', 'NWC'),
                                 feature_group_count=D_C) + conv_b
    h = layer_norm(h, ln_g, ln_b)
    return jax.nn.silu(h)


def grouped_moe(x, router_w, router_bias, w_gate, w_up, w_down):
    B, S, D = x.shape
    N = B * S
    xt = x.reshape(N, D)
    aff = jax.nn.sigmoid(jnp.dot(xt, router_w).astype(jnp.float32))
    sel_score = aff + router_bias.astype(jnp.float32)
    grp_score = lax.top_k(sel_score.reshape(N, N_EXPERT_GROUPS, EXPERTS_PER_GROUP), TOP_K)[0].sum(-1)
    best = jnp.argmax(grp_score, axis=-1)
    in_group = (jnp.arange(N_EXPERTS) // EXPERTS_PER_GROUP)[None, :] == best[:, None]
    _, top_idx = lax.top_k(jnp.where(in_group, sel_score, -jnp.inf), TOP_K)
    top_aff = jnp.take_along_axis(aff, top_idx, axis=-1)
    gates_k = top_aff / jnp.sum(top_aff, axis=-1, keepdims=True)
    gates = jnp.sum(jax.nn.one_hot(top_idx, N_EXPERTS, dtype=jnp.float32) * gates_k[..., None], axis=1)
    chunk = math.gcd(N, MOE_CHUNK)

    def chunk_fn(args):
        xc, gc = args
        h = jax.nn.silu(jnp.einsum('cd,edf->cef', xc, w_gate)) * jnp.einsum('cd,edf->cef', xc, w_up)
        h = h * gc[:, :, None].astype(h.dtype)
        return jnp.einsum('cef,efd->cd', h, w_down)

    y = lax.map(chunk_fn, (xt.reshape(-1, chunk, D), gates.reshape(-1, chunk, N_EXPERTS)))
    return y.reshape(B, S, D)


def setup_inputs(seed: int = 0) -> dict:
    key = jax.random.key(seed)
    ks = jax.random.split(key, 24)
    L = DEPTH
    nrm = lambda k, shape, s: jax.random.normal(k, shape, jnp.float32) * s
    return {
        "x": nrm(ks[0], (BATCH, SEQ, D_MODEL), 1.0),
        "w_in": nrm(ks[1], (L, D_MODEL, D_IN), D_MODEL ** -0.5),
        "kv_norm_g": 1.0 + nrm(ks[2], (L, KV_RANK), 0.02),
        "w_uk": nrm(ks[3], (L, N_HEADS_A, KV_RANK, HEAD_DIM_A), KV_RANK ** -0.5),
        "w_uv": nrm(ks[4], (L, N_HEADS_A, KV_RANK, HEAD_DIM_A), KV_RANK ** -0.5),
        "w_pool": nrm(ks[5], (L, N_POOL_GROUPS, POOL_GROUP_DIM, POOL_GROUP_DIM), POOL_GROUP_DIM ** -0.5),
        "pool_scale": 1.0 + nrm(ks[6], (L, D_B), 0.02),
        "conv_w": nrm(ks[7], (L, CONV_WIDTH, 1, D_C), CONV_WIDTH ** -0.5),
        "conv_b": nrm(ks[8], (L, D_C), 0.02),
        "conv_ln_g": 1.0 + nrm(ks[9], (L, D_C), 0.02),
        "conv_ln_b": nrm(ks[10], (L, D_C), 0.02),
        "w_o": nrm(ks[11], (L, D_MIX, D_MODEL), BETA * D_MIX ** -0.5),
        "ln1_g": 1.0 + nrm(ks[12], (L, D_MODEL), 0.02),
        "ln1_b": nrm(ks[13], (L, D_MODEL), 0.02),
        "router_w": nrm(ks[14], (D_MODEL, N_EXPERTS), D_MODEL ** -0.5),
        "router_bias": nrm(ks[15], (N_EXPERTS,), 0.01),
        "w_gate": nrm(ks[16], (L, N_EXPERTS, D_MODEL, D_EXPERT), D_MODEL ** -0.5),
        "w_up": nrm(ks[17], (L, N_EXPERTS, D_MODEL, D_EXPERT), D_MODEL ** -0.5),
        "w_down": nrm(ks[18], (L, N_EXPERTS, D_EXPERT, D_MODEL), BETA * D_EXPERT ** -0.5),
        "ln2_g": 1.0 + nrm(ks[19], (L, D_MODEL), 0.02),
        "ln2_b": nrm(ks[20], (L, D_MODEL), 0.02),
    }


def reference(x, w_in, kv_norm_g, w_uk, w_uv, w_pool, pool_scale, conv_w, conv_b, conv_ln_g,
              conv_ln_b, w_o, ln1_g, ln1_b, router_w, router_bias, w_gate, w_up, w_down,
              ln2_g, ln2_b):
    B, S = x.shape[0], x.shape[1]
    for l in range(DEPTH):
        proj = jnp.einsum('bsd,de->bse', x, w_in[l])
        q, c_kv, iq, ik, iw, u_pool, u_conv = jnp.split(proj, SPLIT_POINTS, axis=-1)
        q = q.reshape(B, S, N_HEADS_A, HEAD_DIM_A)
        c_kv = rms_norm(c_kv, kv_norm_g[l])
        iq = iq.reshape(B, S, N_IDX_HEADS, IDX_DIM) * (IDX_DIM ** -0.5)
        iw = iw * (N_IDX_HEADS ** -0.5)
        y_a = dsa_attention(q, c_kv, iq, ik, iw, w_uk[l], w_uv[l])
        y_b = multi_scale_pool(u_pool, w_pool[l], pool_scale[l])
        y_c = conformer_conv(u_conv, conv_w[l], conv_b[l], conv_ln_g[l], conv_ln_b[l])
        mix = jnp.einsum('bse,ed->bsd', jnp.concatenate([y_a, y_b, y_c], axis=-1), w_o[l])
        x = layer_norm(ALPHA * x + mix, ln1_g[l], ln1_b[l])
        ffn = grouped_moe(x, router_w, router_bias, w_gate[l], w_up[l], w_down[l])
        x = layer_norm(ALPHA * x + ffn, ln2_g[l], ln2_b[l])
    return x
```

```python
import functools
import math

import jax
import jax.numpy as jnp
from jax import lax
from jax.experimental import pallas as pl
from jax.experimental.pallas import tpu as pltpu

N_HEADS_A = 8
HEAD_DIM_A = 64
D_A = N_HEADS_A * HEAD_DIM_A
KV_RANK = 128
N_IDX_HEADS = 8
IDX_DIM = 32
TOPK_MAX = 256
Q_BLOCK = 128
N_POOL_GROUPS = 4
POOL_GROUP_DIM = 64
D_B = N_POOL_GROUPS * POOL_GROUP_DIM
POOL_WINDOWS = (2, 4, 8, 16)
D_C = 256
CONV_WIDTH = 31
N_EXPERTS = 16
N_EXPERT_GROUPS = 4
EXPERTS_PER_GROUP = N_EXPERTS // N_EXPERT_GROUPS
D_EXPERT = 512
DEPTH = 2
ALPHA = (2 * DEPTH) ** 0.25
LN_EPS = 1e-5

_R_Q = 0
_R_KV = _R_Q + D_A
_R_IQ = _R_KV + KV_RANK
_R_IK = _R_IQ + N_IDX_HEADS * IDX_DIM
_R_IW = _R_IK + IDX_DIM
_R_POOL = _R_IW + N_IDX_HEADS
_R_CONV = _R_POOL + D_B
_R_END = _R_CONV + 2 * D_C

LANES = 128
C_Q = 0
C_KV = C_Q + D_A
C_IDX = C_KV + KV_RANK
C_IQ = C_IDX + LANES
C_UC = C_IQ + N_IDX_HEADS * IDX_DIM
C_UP = C_UC + 2 * D_C
C_END = C_UP + D_B
IW_LANE = 96

MXU_DTYPE = jnp.bfloat16
VMEM_LIMIT = 56 * 1024 * 1024
NEG_BIG = -1e30
KEY_NEG_INF = -2139095041
INT32_MAX = 2147483647
LOG2E = 1.4426950408889634


def _cparams(n_axes):
    return pltpu.CompilerParams(dimension_semantics=("arbitrary",) * n_axes,
                                vmem_limit_bytes=VMEM_LIMIT)


def _layer_norm(z, g, b):
    mu = jnp.mean(z, axis=-1, keepdims=True)
    zc = z - mu
    var = jnp.mean(zc * zc, axis=-1, keepdims=True)
    return zc * lax.rsqrt(var + LN_EPS) * g + b


def _sigmoid(v):
    return 1.0 / (1.0 + jnp.exp(-v))


def _in_kernel(x_ref, w_ref, g_ref, wuk_ref,
               ubc_ref, ckv_ref, ckvT_ref, ik_ref, qlatT_ref, iqT_ref, wT_ref, p_scr):
    tm = x_ref.shape[0]
    p_scr[...] = jnp.dot(x_ref[...].astype(MXU_DTYPE), w_ref[...], preferred_element_type=jnp.float32)

    ubc_ref[...] = p_scr[:, C_UC:C_END]

    c = p_scr[:, C_KV:C_KV + KV_RANK]
    ms = jnp.mean(c * c, axis=-1, keepdims=True)
    cn = c * lax.rsqrt(ms + LN_EPS) * g_ref[...]
    ckv_ref[...] = cn.astype(ckv_ref.dtype)
    ckvT_ref[...] = cn.T.astype(ckvT_ref.dtype)

    idx = p_scr[:, C_IDX:C_IDX + LANES]
    lane = lax.broadcasted_iota(jnp.int32, idx.shape, 1)
    ik_ref[...] = jnp.where(lane < IDX_DIM, idx, 0.0).astype(ik_ref.dtype)

    q_scale = HEAD_DIM_A ** -0.5 * LOG2E
    for j in range(tm // Q_BLOCK):
        r0 = j * Q_BLOCK
        iqb_t = (p_scr[r0:r0 + Q_BLOCK, C_IQ:C_IQ + N_IDX_HEADS * IDX_DIM] * (IDX_DIM ** -0.5)).T
        for h in range(N_IDX_HEADS):
            iqT_ref[j, 0:IDX_DIM, h * LANES:(h + 1) * LANES] = (
                iqb_t[h * IDX_DIM:(h + 1) * IDX_DIM, :].astype(iqT_ref.dtype))
        iqT_ref[j, IDX_DIM:LANES, :] = jnp.zeros((LANES - IDX_DIM, N_IDX_HEADS * LANES), iqT_ref.dtype)
        idx_t = p_scr[r0:r0 + Q_BLOCK, C_IDX:C_IDX + LANES].T
        wT_ref[j] = idx_t[IW_LANE:IW_LANE + N_IDX_HEADS, :] * (N_IDX_HEADS ** -0.5)
        for h in range(N_HEADS_A):
            qh = p_scr[r0:r0 + Q_BLOCK, C_Q + h * HEAD_DIM_A:C_Q + (h + 1) * HEAD_DIM_A].astype(MXU_DTYPE)
            t = lax.dot_general(wuk_ref[h], qh, (((1,), (1,)), ((), ())),
                                preferred_element_type=jnp.float32)
            qlatT_ref[j, :, h * LANES:(h + 1) * LANES] = (t * q_scale).astype(qlatT_ref.dtype)


def _in_call(xf, w_arr, kv_g, w_uk, tm):
    n, d = xf.shape
    nqb = n // Q_BLOCK
    jb = tm // Q_BLOCK
    hq = N_HEADS_A * LANES
    out_shape = (
        jax.ShapeDtypeStruct((n, C_END - C_UC), jnp.float32),
        jax.ShapeDtypeStruct((n, KV_RANK), MXU_DTYPE),
        jax.ShapeDtypeStruct((KV_RANK, n), MXU_DTYPE),
        jax.ShapeDtypeStruct((n, LANES), MXU_DTYPE),
        jax.ShapeDtypeStruct((nqb, KV_RANK, hq), MXU_DTYPE),
        jax.ShapeDtypeStruct((nqb, LANES, hq), MXU_DTYPE),
        jax.ShapeDtypeStruct((nqb, N_IDX_HEADS, LANES), jnp.float32),
    )
    return pl.pallas_call(
        _in_kernel,
        out_shape=out_shape,
        grid=(n // tm,),
        in_specs=[
            pl.BlockSpec((tm, d), lambda i: (i, 0)),
            pl.BlockSpec((d, C_END), lambda i: (0, 0)),
            pl.BlockSpec((1, KV_RANK), lambda i: (0, 0)),
            pl.BlockSpec((N_HEADS_A, KV_RANK, HEAD_DIM_A), lambda i: (0, 0, 0)),
        ],
        out_specs=(
            pl.BlockSpec((tm, C_END - C_UC), lambda i: (i, 0)),
            pl.BlockSpec((tm, KV_RANK), lambda i: (i, 0)),
            pl.BlockSpec((KV_RANK, tm), lambda i: (0, i)),
            pl.BlockSpec((tm, LANES), lambda i: (i, 0)),
            pl.BlockSpec((jb, KV_RANK, hq), lambda i: (i, 0, 0)),
            pl.BlockSpec((jb, LANES, hq), lambda i: (i, 0, 0)),
            pl.BlockSpec((jb, N_IDX_HEADS, LANES), lambda i: (i, 0, 0)),
        ),
        scratch_shapes=[pltpu.VMEM((tm, C_END), jnp.float32)],
        compiler_params=_cparams(1),
        name="in_proj",
    )(xf, w_arr, kv_g, w_uk)


HALO = 32


def _bc_kernel(ubc_ref, wpool_ref, pscale_ref, convw_ref, convb_ref, lng_ref, lnb_ref,
               out_ref, ext_c, ext_p):
    t_idx = pl.program_id(1)
    ts = ubc_ref.shape[1]

    @pl.when(t_idx == 0)
    def _():
        ext_c[0:HALO, :] = jnp.zeros((HALO, D_C), jnp.float32)
        ext_p[0:HALO, :] = jnp.zeros((HALO, D_B), jnp.float32)

    a = ubc_ref[0, :, 0:D_C]
    gate = ubc_ref[0, :, D_C:2 * D_C]
    up = ubc_ref[0, :, 2 * D_C:2 * D_C + D_B]
    ext_c[HALO:HALO + ts, :] = a * _sigmoid(gate)
    ext_p[HALO:HALO + ts, :] = up

    base = HALO - (CONV_WIDTH - 1)
    conv = jnp.zeros((ts, D_C), jnp.float32) + convb_ref[...]
    for k in range(CONV_WIDTH):
        conv = conv + ext_c[base + k:base + k + ts, :] * convw_ref[k:k + 1, :]
    hc = _layer_norm(conv, lng_ref[...], lnb_ref[...])
    yc = hc * _sigmoid(hc)

    def shifted(j):
        return ext_p[HALO - j:HALO - j + ts, :]
    s2 = shifted(0) + shifted(1)
    s4 = s2 + (shifted(2) + shifted(3))
    s8 = s4 + ((shifted(4) + shifted(5)) + (shifted(6) + shifted(7)))
    s16 = s8 + (((shifted(8) + shifted(9)) + (shifted(10) + shifted(11)))
                + ((shifted(12) + shifted(13)) + (shifted(14) + shifted(15))))
    grp = lax.broadcasted_iota(jnp.int32, (ts, D_B), 1) // POOL_GROUP_DIM
    ssel = jnp.where(grp == 0, s2, jnp.where(grp == 1, s4, jnp.where(grp == 2, s8, s16)))
    wlane = jnp.where(grp == 0, 2.0, jnp.where(grp == 1, 4.0, jnp.where(grp == 2, 8.0, 16.0)))
    tpos = (t_idx * ts + lax.broadcasted_iota(jnp.int32, (ts, D_B), 0) + 1).astype(jnp.float32)
    cnt = jnp.minimum(tpos, wlane)
    pooled = ssel / cnt - up
    yb = jnp.dot(pooled.astype(MXU_DTYPE), wpool_ref[...], preferred_element_type=jnp.float32) * pscale_ref[...]

    out_ref[0, :, 0:D_B] = yb.astype(out_ref.dtype)
    out_ref[0, :, D_B:D_B + D_C] = yc.astype(out_ref.dtype)

    ext_c[0:HALO, :] = ext_c[ts:ts + HALO, :]
    ext_p[0:HALO, :] = ext_p[ts:ts + HALO, :]


def _bc_call(ubc, wpool_bd, pscale, convw, convb, lng, lnb, ts):
    b, s, _ = ubc.shape
    full2 = lambda bi, ti: (0, 0)
    return pl.pallas_call(
        _bc_kernel,
        out_shape=jax.ShapeDtypeStruct((b, s, D_B + D_C), MXU_DTYPE),
        grid=(b, s // ts),
        in_specs=[
            pl.BlockSpec((1, ts, C_END - C_UC), lambda bi, ti: (bi, ti, 0)),
            pl.BlockSpec((D_B, D_B), full2),
            pl.BlockSpec((1, D_B), full2),
            pl.BlockSpec((HALO, D_C), full2),
            pl.BlockSpec((1, D_C), full2),
            pl.BlockSpec((1, D_C), full2),
            pl.BlockSpec((1, D_C), full2),
        ],
        out_specs=pl.BlockSpec((1, ts, D_B + D_C), lambda bi, ti: (bi, ti, 0)),
        scratch_shapes=[pltpu.VMEM((ts + HALO, D_C), jnp.float32),
                        pltpu.VMEM((ts + HALO, D_B), jnp.float32)],
        compiler_params=_cparams(2),
        name="pool_conv",
    )(ubc, wpool_bd, pscale, convw, convb, lng, lnb)


KC = 512


def _dsa_kernel(ik_ref, ckv_ref, ckvT_ref, qlatT_ref, iqT_ref, wT_ref, wuvT_ref,
                ya_ref, keys_scr, acc_scr, m_scr, l_scr, *, topk):
    i = pl.program_id(1)
    qpb = KC // Q_BLOCK
    nch = (i + qpb) // qpb
    hq = N_HEADS_A * LANES
    q_pos = i * Q_BLOCK + lax.broadcasted_iota(jnp.int32, (1, LANES), 1)
    target = jnp.minimum(topk, q_pos + 1)

    def idx_body(c, carry):
        off = pl.multiple_of(c * KC, KC)
        lt = jnp.dot(ik_ref[0, pl.ds(off, KC), :], iqT_ref[0], preferred_element_type=jnp.float32)
        sc = jnp.zeros((KC, LANES), jnp.float32)
        for h in range(N_IDX_HEADS):
            sc = sc + jnp.maximum(lt[:, h * LANES:(h + 1) * LANES], 0.0) * wT_ref[0, h:h + 1, :]
        sc = jnp.where(sc == 0.0, 0.0, sc)
        kpos = off + lax.broadcasted_iota(jnp.int32, (KC, LANES), 0)
        sc = jnp.where(kpos <= q_pos, sc, -jnp.inf)
        bits = pltpu.bitcast(sc, jnp.int32)
        keys_scr[pl.ds(off, KC), :] = jnp.where(bits < 0, bits ^ INT32_MAX, bits)
        return carry

    lax.fori_loop(0, nch, idx_body, 0)

    def count_ge(thr):
        def body(c, acc):
            off = pl.multiple_of(c * KC, KC)
            ind = jnp.where(keys_scr[pl.ds(off, KC), :] >= thr, 1, 0).astype(jnp.int32)
            return acc + jnp.sum(ind.reshape(KC // 8, 8, LANES), axis=0)
        acc = lax.fori_loop(0, nch, body, jnp.zeros((8, LANES), jnp.int32))
        return jnp.sum(acc, axis=0, keepdims=True)

    def avg_floor(a, b):
        return (a >> 1) + (b >> 1) + (a & b & 1)

    def sel_cond(state):
        return state[3] > 0

    def sel_body(state):
        lo, hi, cnt_lo, _ = state
        mid = avg_floor(lo, hi)
        active = (cnt_lo != target) & (mid != lo)
        c = count_ge(mid)
        ge = c >= target
        up = active & ge
        dn = active & jnp.logical_not(ge)
        lo = jnp.where(up, mid, lo)
        cnt_lo = jnp.where(up, c, cnt_lo)
        hi = jnp.where(dn, mid, hi)
        still = (cnt_lo != target) & (avg_floor(lo, hi) != lo)
        return lo, hi, cnt_lo, jnp.max(still.astype(jnp.int32))

    lo0 = jnp.full((1, LANES), KEY_NEG_INF + 1, jnp.int32)
    hi0 = jnp.full((1, LANES), INT32_MAX, jnp.int32)
    cnt0 = q_pos + 1
    go0 = jnp.max((cnt0 != target).astype(jnp.int32))
    lo, _, _, _ = lax.while_loop(sel_cond, sel_body, (lo0, hi0, cnt0, go0))

    m_scr[...] = jnp.full(m_scr.shape, NEG_BIG, jnp.float32)
    l_scr[...] = jnp.zeros(l_scr.shape, jnp.float32)
    acc_scr[...] = jnp.zeros(acc_scr.shape, jnp.float32)

    def att_body(c, carry):
        off = pl.multiple_of(c * KC, KC)
        bias = jnp.where(keys_scr[pl.ds(off, KC), :] >= lo, 0.0, NEG_BIG)
        st = jnp.dot(ckv_ref[0, pl.ds(off, KC), :], qlatT_ref[0], preferred_element_type=jnp.float32)
        ps = []
        alphas = []
        for h in range(N_HEADS_A):
            sl = slice(h * LANES, (h + 1) * LANES)
            s = st[:, sl] + bias
            m_old = m_scr[:, sl]
            m_new = jnp.maximum(m_old, jnp.max(s, axis=0, keepdims=True))
            alpha = jnp.exp2(m_old - m_new)
            p = jnp.exp2(s - m_new)
            l_scr[:, sl] = alpha * l_scr[:, sl] + jnp.sum(p, axis=0, keepdims=True)
            m_scr[:, sl] = m_new
            ps.append(p.astype(MXU_DTYPE))
            alphas.append(alpha)
        p_all = jnp.concatenate(ps, axis=1)
        alpha_all = jnp.concatenate(alphas, axis=1)
        pv = jnp.dot(ckvT_ref[:, pl.ds(off, KC)], p_all, preferred_element_type=jnp.float32)
        acc_scr[...] = acc_scr[...] * alpha_all + pv
        return carry

    lax.fori_loop(0, nch, att_body, 0)

    o_t = (acc_scr[...] * (1.0 / l_scr[...])).astype(MXU_DTYPE)
    ys = []
    for h in range(N_HEADS_A):
        ys.append(jnp.dot(wuvT_ref[h], o_t[:, h * LANES:(h + 1) * LANES],
                          preferred_element_type=jnp.float32))
    ya_ref[0] = jnp.concatenate(ys, axis=0).T.astype(ya_ref.dtype)


def _dsa_call(ik, ckv, ckvT, qlatT, iqT, wT, wuvT, b, s):
    nb = s // Q_BLOCK
    hq = N_HEADS_A * LANES
    topk = min(TOPK_MAX, s // 4)
    return pl.pallas_call(
        functools.partial(_dsa_kernel, topk=topk),
        out_shape=jax.ShapeDtypeStruct((b, s, D_A), MXU_DTYPE),
        grid=(b, nb),
        in_specs=[
            pl.BlockSpec((1, s, LANES), lambda bi, qi: (bi, 0, 0)),
            pl.BlockSpec((1, s, KV_RANK), lambda bi, qi: (bi, 0, 0)),
            pl.BlockSpec((KV_RANK, s), lambda bi, qi: (0, bi)),
            pl.BlockSpec((1, KV_RANK, hq), lambda bi, qi: (bi * nb + qi, 0, 0)),
            pl.BlockSpec((1, LANES, hq), lambda bi, qi: (bi * nb + qi, 0, 0)),
            pl.BlockSpec((1, N_IDX_HEADS, LANES), lambda bi, qi: (bi * nb + qi, 0, 0)),
            pl.BlockSpec((N_HEADS_A, HEAD_DIM_A, KV_RANK), lambda bi, qi: (0, 0, 0)),
        ],
        out_specs=pl.BlockSpec((1, Q_BLOCK, D_A), lambda bi, qi: (bi, qi, 0)),
        scratch_shapes=[
            pltpu.VMEM((s, LANES), jnp.int32),
            pltpu.VMEM((KV_RANK, hq), jnp.float32),
            pltpu.VMEM((1, hq), jnp.float32),
            pltpu.VMEM((1, hq), jnp.float32),
        ],
        compiler_params=_cparams(2),
        name="dsa_attention",
    )(ik, ckv, ckvT, qlatT, iqT, wT, wuvT)


def _out_kernel(x_ref, ya_ref, ybc_ref, wo_ref, g_ref, b_ref, rwT_ref, rbias_ref,
                x1_ref, gatesT_ref, sel_scr, aff_scr):
    mix = jnp.dot(ya_ref[...], wo_ref[0:D_A, :], preferred_element_type=jnp.float32)
    mix = mix + jnp.dot(ybc_ref[...], wo_ref[D_A:, :], preferred_element_type=jnp.float32)
    x1 = _layer_norm(ALPHA * x_ref[...] + mix, g_ref[...], b_ref[...])
    x1_ref[...] = x1

    logits_t = lax.dot_general(rwT_ref[...], x1, (((1,), (1,)), ((), ())),
                               precision=lax.Precision.HIGHEST,
                               preferred_element_type=jnp.float32)
    aff = _sigmoid(logits_t)
    aff_scr[...] = aff
    sel_scr[...] = aff + rbias_ref[...]

    def row(ref, e):
        return ref[e:e + 1, :]

    grp_scores = []
    for g in range(N_EXPERT_GROUPS):
        r = [row(sel_scr, g * EXPERTS_PER_GROUP + k) for k in range(EXPERTS_PER_GROUP)]
        m1, n1 = jnp.maximum(r[0], r[1]), jnp.minimum(r[0], r[1])
        m2, n2 = jnp.maximum(r[2], r[3]), jnp.minimum(r[2], r[3])
        grp_scores.append(jnp.maximum(m1, m2) + jnp.maximum(jnp.minimum(m1, m2), jnp.maximum(n1, n2)))
    best = jnp.zeros_like(grp_scores[0], dtype=jnp.int32)
    cur = grp_scores[0]
    for g in range(1, N_EXPERT_GROUPS):
        better = grp_scores[g] > cur
        best = jnp.where(better, g, best)
        cur = jnp.where(better, grp_scores[g], cur)

    sel_aff = []
    for e in range(N_EXPERTS):
        g = e // EXPERTS_PER_GROUP
        se = row(sel_scr, e)
        rank = jnp.zeros_like(best)
        for f in range(g * EXPERTS_PER_GROUP, (g + 1) * EXPERTS_PER_GROUP):
            if f == e:
                continue
            sf = row(sel_scr, f)
            ahead = (sf > se) | ((sf == se) & (f < e))
            rank = rank + ahead.astype(jnp.int32)
        chosen = (best == g) & (rank < 2)
        sel_aff.append(jnp.where(chosen, row(aff_scr, e), 0.0))
    denom = sel_aff[0]
    for e in range(1, N_EXPERTS):
        denom = denom + sel_aff[e]
    inv = 1.0 / denom
    for e in range(N_EXPERTS):
        gatesT_ref[e:e + 1, :] = sel_aff[e] * inv


def _out_call(xf, ya, ybc, wo, g, b, rwT, rbias, tm):
    n, d = xf.shape
    full = lambda i: (0, 0)
    return pl.pallas_call(
        _out_kernel,
        out_shape=(jax.ShapeDtypeStruct((n, d), jnp.float32),
                   jax.ShapeDtypeStruct((N_EXPERTS, n), jnp.float32)),
        grid=(n // tm,),
        in_specs=[
            pl.BlockSpec((tm, d), lambda i: (i, 0)),
            pl.BlockSpec((tm, D_A), lambda i: (i, 0)),
            pl.BlockSpec((tm, D_B + D_C), lambda i: (i, 0)),
            pl.BlockSpec((d, d), full),
            pl.BlockSpec((1, d), full),
            pl.BlockSpec((1, d), full),
            pl.BlockSpec((N_EXPERTS, d), full),
            pl.BlockSpec((N_EXPERTS, 1), full),
        ],
        out_specs=(pl.BlockSpec((tm, d), lambda i: (i, 0)),
                   pl.BlockSpec((N_EXPERTS, tm), lambda i: (0, i))),
        scratch_shapes=[pltpu.VMEM((N_EXPERTS, tm), jnp.float32),
                        pltpu.VMEM((N_EXPERTS, tm), jnp.float32)],
        compiler_params=_cparams(1),
        name="out_proj_router",
    )(xf, ya, ybc, wo, g, b, rwT, rbias)


def _moe_kernel(x1_ref, gates_ref, wg_ref, wu_ref, wd_ref, g_ref, b_ref, out_ref, xb_scr, acc_scr):
    e = pl.program_id(1)

    @pl.when(e == 0)
    def _():
        xb_scr[...] = x1_ref[...].astype(xb_scr.dtype)
        acc_scr[...] = jnp.zeros(acc_scr.shape, jnp.float32)

    xb = xb_scr[...]
    hg = jnp.dot(xb, wg_ref[0], preferred_element_type=jnp.float32)
    hu = jnp.dot(xb, wu_ref[0], preferred_element_type=jnp.float32)
    gates = gates_ref[...]
    lane = lax.broadcasted_iota(jnp.int32, gates.shape, 1)
    gcol = jnp.sum(jnp.where(lane == e, gates, 0.0), axis=1, keepdims=True)
    h = (hg * _sigmoid(hg)) * hu * gcol
    acc_scr[...] += jnp.dot(h.astype(MXU_DTYPE), wd_ref[0], preferred_element_type=jnp.float32)

    @pl.when(e == pl.num_programs(1) - 1)
    def _():
        out_ref[...] = _layer_norm(ALPHA * x1_ref[...] + acc_scr[...], g_ref[...], b_ref[...])


def _moe_call(x1, gates, wg, wu, wd, g, b, tm):
    n, d = x1.shape
    ne, _, de = wg.shape
    return pl.pallas_call(
        _moe_kernel,
        out_shape=jax.ShapeDtypeStruct((n, d), jnp.float32),
        grid=(n // tm, ne),
        in_specs=[
            pl.BlockSpec((tm, d), lambda i, e: (i, 0)),
            pl.BlockSpec((tm, ne), lambda i, e: (i, 0)),
            pl.BlockSpec((1, d, de), lambda i, e: (e, 0, 0)),
            pl.BlockSpec((1, d, de), lambda i, e: (e, 0, 0)),
            pl.BlockSpec((1, de, d), lambda i, e: (e, 0, 0)),
            pl.BlockSpec((1, d), lambda i, e: (0, 0)),
            pl.BlockSpec((1, d), lambda i, e: (0, 0)),
        ],
        out_specs=pl.BlockSpec((tm, d), lambda i, e: (i, 0)),
        scratch_shapes=[pltpu.VMEM((tm, d), MXU_DTYPE), pltpu.VMEM((tm, d), jnp.float32)],
        compiler_params=_cparams(2),
        name="moe_ffn",
    )(x1, gates, wg, wu, wd, g, b)


def _arrange_w_in(w):
    d = w.shape[0]
    z = lambda k: jnp.zeros((d, k), w.dtype)
    idx_block = jnp.concatenate([w[:, _R_IK:_R_IW], z(IW_LANE - IDX_DIM), w[:, _R_IW:_R_POOL],
                                 z(LANES - IW_LANE - N_IDX_HEADS)], axis=1)
    return jnp.concatenate([w[:, _R_Q:_R_KV], w[:, _R_KV:_R_IQ], idx_block, w[:, _R_IQ:_R_IK],
                            w[:, _R_CONV:_R_END], w[:, _R_POOL:_R_CONV]], axis=1)


def _block_diag(w_pool):
    g, c, _ = w_pool.shape
    out = jnp.zeros((g * c, g * c), w_pool.dtype)
    for gi in range(g):
        out = out.at[gi * c:(gi + 1) * c, gi * c:(gi + 1) * c].set(w_pool[gi])
    return out


def _pick(n, pref):
    t = pref
    while n % t:
        t //= 2
    return t


def kernel(x, w_in, kv_norm_g, w_uk, w_uv, w_pool, pool_scale, conv_w, conv_b, conv_ln_g,
           conv_ln_b, w_o, ln1_g, ln1_b, router_w, router_bias, w_gate, w_up, w_down,
           ln2_g, ln2_b):
    b, s, d = x.shape
    n = b * s
    depth = w_in.shape[0]
    assert s % KC == 0 and d == 1024
    xf = x.reshape(n, d)
    rwT = router_w.T
    rbias = router_bias.reshape(N_EXPERTS, 1)
    tm_in = _pick(n, 512)
    ts_bc = _pick(s, 512)
    tm_out = _pick(n, 512)
    tm_moe = _pick(n, 1024)
    for l in range(depth):
        w_arr = _arrange_w_in(w_in[l]).astype(MXU_DTYPE)
        ubc, ckv, ckvT, ik, qlatT, iqT, wT = _in_call(
            xf, w_arr, kv_norm_g[l].reshape(1, KV_RANK), w_uk[l].astype(MXU_DTYPE), tm_in)
        convw = jnp.concatenate([conv_w[l].reshape(CONV_WIDTH, D_C),
                                 jnp.zeros((HALO - CONV_WIDTH, D_C), conv_w.dtype)], axis=0)
        ybc = _bc_call(ubc.reshape(b, s, C_END - C_UC), _block_diag(w_pool[l]).astype(MXU_DTYPE),
                       pool_scale[l].reshape(1, D_B), convw, conv_b[l].reshape(1, D_C),
                       conv_ln_g[l].reshape(1, D_C), conv_ln_b[l].reshape(1, D_C), ts_bc)
        wuvT = jnp.swapaxes(w_uv[l], 1, 2).astype(MXU_DTYPE)
        ya = _dsa_call(ik.reshape(b, s, LANES), ckv.reshape(b, s, KV_RANK), ckvT, qlatT, iqT, wT, wuvT, b, s)
        x1, gates_t = _out_call(xf, ya.reshape(n, D_A), ybc.reshape(n, D_B + D_C), w_o[l].astype(MXU_DTYPE),
                                ln1_g[l].reshape(1, d), ln1_b[l].reshape(1, d), rwT, rbias, tm_out)
        xf = _moe_call(x1, gates_t.T, w_gate[l].astype(MXU_DTYPE), w_up[l].astype(MXU_DTYPE),
                       w_down[l].astype(MXU_DTYPE), ln2_g[l].reshape(1, d), ln2_b[l].reshape(1, d), tm_moe)
    return xf.reshape(b, s, d)
```

```python
import functools
import math

import jax
import jax.numpy as jnp
from jax import lax
from jax.experimental import pallas as pl
from jax.experimental.pallas import tpu as pltpu

N_HEADS_A = 8
HEAD_DIM_A = 64
D_A = N_HEADS_A * HEAD_DIM_A
KV_RANK = 128
N_IDX_HEADS = 8
IDX_DIM = 32
TOPK_MAX = 256
Q_BLOCK = 128
N_POOL_GROUPS = 4
POOL_GROUP_DIM = 64
D_B = N_POOL_GROUPS * POOL_GROUP_DIM
POOL_WINDOWS = (2, 4, 8, 16)
D_C = 256
CONV_WIDTH = 31
N_EXPERTS = 16
N_EXPERT_GROUPS = 4
EXPERTS_PER_GROUP = N_EXPERTS // N_EXPERT_GROUPS
D_EXPERT = 512
DEPTH = 2
ALPHA = (2 * DEPTH) ** 0.25
LN_EPS = 1e-5

_R_Q = 0
_R_KV = _R_Q + D_A
_R_IQ = _R_KV + KV_RANK
_R_IK = _R_IQ + N_IDX_HEADS * IDX_DIM
_R_IW = _R_IK + IDX_DIM
_R_POOL = _R_IW + N_IDX_HEADS
_R_CONV = _R_POOL + D_B
_R_END = _R_CONV + 2 * D_C

LANES = 128
C_Q = 0
C_KV = C_Q + D_A
C_IDX = C_KV + KV_RANK
C_IQ = C_IDX + LANES
C_UC = C_IQ + N_IDX_HEADS * IDX_DIM
C_UP = C_UC + 2 * D_C
C_END = C_UP + D_B
IW_LANE = 96

MXU_DTYPE = jnp.bfloat16
VMEM_LIMIT = 56 * 1024 * 1024
NEG_BIG = -1e30
LOG2E = 1.4426950408889634


def _cparams(n_axes):
    return pltpu.CompilerParams(dimension_semantics=("arbitrary",) * n_axes,
                                vmem_limit_bytes=VMEM_LIMIT)


def _layer_norm(z, g, b):
    mu = jnp.mean(z, axis=-1, keepdims=True)
    zc = z - mu
    var = jnp.mean(zc * zc, axis=-1, keepdims=True)
    return zc * lax.rsqrt(var + LN_EPS) * g + b


def _sigmoid(v):
    return 1.0 / (1.0 + jnp.exp(-v))


def _in_kernel(x_ref, w_ref, g_ref, wuk_ref,
               ubc_ref, ckv_ref, ckvT_ref, ik_ref, qlatT_ref, iqT_ref, wT_ref, p_scr):
    tm = x_ref.shape[0]
    p_scr[...] = jnp.dot(x_ref[...].astype(MXU_DTYPE), w_ref[...], preferred_element_type=jnp.float32)

    ubc_ref[...] = p_scr[:, C_UC:C_END]

    c = p_scr[:, C_KV:C_KV + KV_RANK]
    ms = jnp.mean(c * c, axis=-1, keepdims=True)
    cn = c * lax.rsqrt(ms + LN_EPS) * g_ref[...]
    ckv_ref[...] = cn.astype(ckv_ref.dtype)
    ckvT_ref[...] = cn.T.astype(ckvT_ref.dtype)

    idx = p_scr[:, C_IDX:C_IDX + LANES]
    lane = lax.broadcasted_iota(jnp.int32, idx.shape, 1)
    ik_ref[...] = jnp.where(lane < IDX_DIM, idx, 0.0).astype(ik_ref.dtype)

    q_scale = HEAD_DIM_A ** -0.5 * LOG2E
    for j in range(tm // Q_BLOCK):
        r0 = j * Q_BLOCK
        iqb_t = (p_scr[r0:r0 + Q_BLOCK, C_IQ:C_IQ + N_IDX_HEADS * IDX_DIM] * (IDX_DIM ** -0.5)).T
        for h in range(N_IDX_HEADS):
            iqT_ref[j, 0:IDX_DIM, h * LANES:(h + 1) * LANES] = (
                iqb_t[h * IDX_DIM:(h + 1) * IDX_DIM, :].astype(iqT_ref.dtype))
        iqT_ref[j, IDX_DIM:LANES, :] = jnp.zeros((LANES - IDX_DIM, N_IDX_HEADS * LANES), iqT_ref.dtype)
        idx_t = p_scr[r0:r0 + Q_BLOCK, C_IDX:C_IDX + LANES].T
        wT_ref[j] = idx_t[IW_LANE:IW_LANE + N_IDX_HEADS, :] * (N_IDX_HEADS ** -0.5)
        for h in range(N_HEADS_A):
            qh = p_scr[r0:r0 + Q_BLOCK, C_Q + h * HEAD_DIM_A:C_Q + (h + 1) * HEAD_DIM_A].astype(MXU_DTYPE)
            t = lax.dot_general(wuk_ref[h], qh, (((1,), (1,)), ((), ())),
                                preferred_element_type=jnp.float32)
            qlatT_ref[j, :, h * LANES:(h + 1) * LANES] = (t * q_scale).astype(qlatT_ref.dtype)


def _in_call(xf, w_arr, kv_g, w_uk, tm):
    n, d = xf.shape
    nqb = n // Q_BLOCK
    jb = tm // Q_BLOCK
    hq = N_HEADS_A * LANES
    out_shape = (
        jax.ShapeDtypeStruct((n, C_END - C_UC), jnp.float32),
        jax.ShapeDtypeStruct((n, KV_RANK), MXU_DTYPE),
        jax.ShapeDtypeStruct((KV_RANK, n), MXU_DTYPE),
        jax.ShapeDtypeStruct((n, LANES), MXU_DTYPE),
        jax.ShapeDtypeStruct((nqb, KV_RANK, hq), MXU_DTYPE),
        jax.ShapeDtypeStruct((nqb, LANES, hq), MXU_DTYPE),
        jax.ShapeDtypeStruct((nqb, N_IDX_HEADS, LANES), jnp.float32),
    )
    return pl.pallas_call(
        _in_kernel,
        out_shape=out_shape,
        grid=(n // tm,),
        in_specs=[
            pl.BlockSpec((tm, d), lambda i: (i, 0)),
            pl.BlockSpec((d, C_END), lambda i: (0, 0)),
            pl.BlockSpec((1, KV_RANK), lambda i: (0, 0)),
            pl.BlockSpec((N_HEADS_A, KV_RANK, HEAD_DIM_A), lambda i: (0, 0, 0)),
        ],
        out_specs=(
            pl.BlockSpec((tm, C_END - C_UC), lambda i: (i, 0)),
            pl.BlockSpec((tm, KV_RANK), lambda i: (i, 0)),
            pl.BlockSpec((KV_RANK, tm), lambda i: (0, i)),
            pl.BlockSpec((tm, LANES), lambda i: (i, 0)),
            pl.BlockSpec((jb, KV_RANK, hq), lambda i: (i, 0, 0)),
            pl.BlockSpec((jb, LANES, hq), lambda i: (i, 0, 0)),
            pl.BlockSpec((jb, N_IDX_HEADS, LANES), lambda i: (i, 0, 0)),
        ),
        scratch_shapes=[pltpu.VMEM((tm, C_END), jnp.float32)],
        compiler_params=_cparams(1),
        name="in_proj",
    )(xf, w_arr, kv_g, w_uk)


HALO = 32


def _bc_kernel(ubc_ref, wpool_ref, pscale_ref, convw_ref, convb_ref, lng_ref, lnb_ref,
               out_ref, ext_c, ext_p):
    t_idx = pl.program_id(1)
    ts = ubc_ref.shape[1]

    @pl.when(t_idx == 0)
    def _():
        ext_c[0:HALO, :] = jnp.zeros((HALO, D_C), jnp.float32)
        ext_p[0:HALO, :] = jnp.zeros((HALO, D_B), jnp.float32)

    a = ubc_ref[0, :, 0:D_C]
    gate = ubc_ref[0, :, D_C:2 * D_C]
    up = ubc_ref[0, :, 2 * D_C:2 * D_C + D_B]
    ext_c[HALO:HALO + ts, :] = a * _sigmoid(gate)
    ext_p[HALO:HALO + ts, :] = up

    base = HALO - (CONV_WIDTH - 1)
    conv = jnp.zeros((ts, D_C), jnp.float32) + convb_ref[...]
    for k in range(CONV_WIDTH):
        conv = conv + ext_c[base + k:base + k + ts, :] * convw_ref[k:k + 1, :]
    hc = _layer_norm(conv, lng_ref[...], lnb_ref[...])
    yc = hc * _sigmoid(hc)

    def shifted(j):
        return ext_p[HALO - j:HALO - j + ts, :]
    s2 = shifted(0) + shifted(1)
    s4 = s2 + (shifted(2) + shifted(3))
    s8 = s4 + ((shifted(4) + shifted(5)) + (shifted(6) + shifted(7)))
    s16 = s8 + (((shifted(8) + shifted(9)) + (shifted(10) + shifted(11)))
                + ((shifted(12) + shifted(13)) + (shifted(14) + shifted(15))))
    grp = lax.broadcasted_iota(jnp.int32, (ts, D_B), 1) // POOL_GROUP_DIM
    ssel = jnp.where(grp == 0, s2, jnp.where(grp == 1, s4, jnp.where(grp == 2, s8, s16)))
    wlane = jnp.where(grp == 0, 2.0, jnp.where(grp == 1, 4.0, jnp.where(grp == 2, 8.0, 16.0)))
    tpos = (t_idx * ts + lax.broadcasted_iota(jnp.int32, (ts, D_B), 0) + 1).astype(jnp.float32)
    cnt = jnp.minimum(tpos, wlane)
    pooled = ssel / cnt - up
    yb = jnp.dot(pooled.astype(MXU_DTYPE), wpool_ref[...], preferred_element_type=jnp.float32) * pscale_ref[...]

    out_ref[0, :, 0:D_B] = yb.astype(out_ref.dtype)
    out_ref[0, :, D_B:D_B + D_C] = yc.astype(out_ref.dtype)

    ext_c[0:HALO, :] = ext_c[ts:ts + HALO, :]
    ext_p[0:HALO, :] = ext_p[ts:ts + HALO, :]


def _bc_call(ubc, wpool_bd, pscale, convw, convb, lng, lnb, ts):
    b, s, _ = ubc.shape
    full2 = lambda bi, ti: (0, 0)
    return pl.pallas_call(
        _bc_kernel,
        out_shape=jax.ShapeDtypeStruct((b, s, D_B + D_C), MXU_DTYPE),
        grid=(b, s // ts),
        in_specs=[
            pl.BlockSpec((1, ts, C_END - C_UC), lambda bi, ti: (bi, ti, 0)),
            pl.BlockSpec((D_B, D_B), full2),
            pl.BlockSpec((1, D_B), full2),
            pl.BlockSpec((HALO, D_C), full2),
            pl.BlockSpec((1, D_C), full2),
            pl.BlockSpec((1, D_C), full2),
            pl.BlockSpec((1, D_C), full2),
        ],
        out_specs=pl.BlockSpec((1, ts, D_B + D_C), lambda bi, ti: (bi, ti, 0)),
        scratch_shapes=[pltpu.VMEM((ts + HALO, D_C), jnp.float32),
                        pltpu.VMEM((ts + HALO, D_B), jnp.float32)],
        compiler_params=_cparams(2),
        name="pool_conv",
    )(ubc, wpool_bd, pscale, convw, convb, lng, lnb)


KC = 512
SEL_FIXED_PASSES = 14


def _dsa_kernel(ik_ref, ckv_ref, ckvT_ref, qlatT_ref, iqT_ref, wT_ref, wuvT_ref,
                ya_ref, sc_scr, acc_scr, m_scr, l_scr, *, topk):
    i = pl.program_id(1)
    qpb = KC // Q_BLOCK
    nch = (i + qpb) // qpb
    q_pos = i * Q_BLOCK + lax.broadcasted_iota(jnp.int32, (1, LANES), 1)
    n_causal = q_pos + 1
    target = jnp.minimum(topk, n_causal)

    def fold8(v, op):
        return op(v.reshape(KC // 8, 8, LANES), axis=0)

    def chunk_off(c):
        return pl.multiple_of(c * KC, KC)

    def key_pos(off):
        return off + lax.broadcasted_iota(jnp.int32, (KC, LANES), 0)

    def idx_chunk(c, stats, masked):
        off = chunk_off(c)
        lt = jnp.dot(ik_ref[0, pl.ds(off, KC), :], iqT_ref[0], preferred_element_type=jnp.float32)
        sc = jnp.zeros((KC, LANES), jnp.float32)
        for h in range(N_IDX_HEADS):
            sc = sc + jnp.maximum(lt[:, h * LANES:(h + 1) * LANES], 0.0) * wT_ref[0, h:h + 1, :]
        sc = jnp.where(sc == 0.0, 0.0, sc)
        sc_for_min = sc
        if masked:
            causal = key_pos(off) <= q_pos
            sc_for_min = jnp.where(causal, sc, jnp.inf)
            sc = jnp.where(causal, sc, -jnp.inf)
        sc_scr[pl.ds(off, KC), :] = sc
        mx, mn, ge0, gt0 = stats
        return (jnp.maximum(mx, fold8(sc, jnp.max)),
                jnp.minimum(mn, fold8(sc_for_min, jnp.min)),
                ge0 + fold8(jnp.where(sc >= 0.0, 1, 0).astype(jnp.int32), jnp.sum),
                gt0 + fold8(jnp.where(sc > 0.0, 1, 0).astype(jnp.int32), jnp.sum))

    stats = (jnp.full((8, LANES), -jnp.inf, jnp.float32), jnp.full((8, LANES), jnp.inf, jnp.float32),
             jnp.zeros((8, LANES), jnp.int32), jnp.zeros((8, LANES), jnp.int32))
    stats = lax.fori_loop(0, nch - 1, lambda c, st: idx_chunk(c, st, False), stats)
    stats = idx_chunk(nch - 1, stats, True)
    rmax = jnp.max(stats[0], axis=0, keepdims=True)
    rmin = jnp.min(stats[1], axis=0, keepdims=True)
    c_ge0 = jnp.sum(stats[2], axis=0, keepdims=True)
    c_gt0 = jnp.sum(stats[3], axis=0, keepdims=True)

    def count_where(pred):
        def body(c, acc):
            off = chunk_off(c)
            ind = jnp.where(pred(sc_scr[pl.ds(off, KC), :], off), 1, 0).astype(jnp.int32)
            return acc + fold8(ind, jnp.sum)
        acc = lax.fori_loop(0, nch, body, jnp.zeros((8, LANES), jnp.int32))
        return jnp.sum(acc, axis=0, keepdims=True)

    def unfinished(c_lo, fin):
        return (c_lo != target) & (fin == 0)

    def sel_pass(state):
        lo, hi, c_lo, fin = state
        mid = 0.5 * lo + 0.5 * hi
        mid = jnp.where(mid <= lo, hi, mid)
        active = unfinished(c_lo, fin)
        c = count_where(lambda blk, off: blk >= mid)
        ge = c >= target
        up = active & ge
        dn = active & jnp.logical_not(ge)
        fin = jnp.where(active & (mid == hi), 1, fin)
        return jnp.where(up, mid, lo), jnp.where(dn, mid, hi), jnp.where(up, c, c_lo), fin

    def any_lane(mask):
        return jnp.max(mask.astype(jnp.int32))

    at_zero = (c_gt0 < target) & (target <= c_ge0)
    above = c_gt0 >= target
    zero = jnp.zeros((1, LANES), jnp.float32)
    lo0 = jnp.where(at_zero | above, zero, rmin)
    hi0 = jnp.where(above, rmax, zero)
    c_lo0 = jnp.where(at_zero | above, c_ge0, n_causal)
    state0 = (lo0, hi0, c_lo0, at_zero.astype(jnp.int32))
    go0 = any_lane(unfinished(state0[2], state0[3]))
    state = lax.cond(go0 > 0,
                     lambda st: lax.fori_loop(0, SEL_FIXED_PASSES, lambda _, s_: sel_pass(s_), st),
                     lambda st: st, state0)

    def sel_more(carry):
        st = sel_pass(carry[0])
        return st, any_lane(unfinished(st[2], st[3]))

    state, _ = lax.while_loop(lambda carry: carry[1] > 0, sel_more,
                              (state, any_lane(unfinished(state[2], state[3]))))
    lo, _, c_lo, _ = state

    tie = c_lo > target

    @pl.when(any_lane(tie) > 0)
    def _():
        need = target - count_where(lambda blk, off: blk > lo)

        def j_body(_, st):
            j_lo, j_hi = st
            j_mid = (j_lo + j_hi) >> 1
            f = count_where(lambda blk, off: (blk == lo) & (key_pos(off) < j_mid))
            ok = f >= need
            return jnp.where(ok, j_lo, j_mid), jnp.where(ok, j_mid, j_hi)

        n_steps = int(sc_scr.shape[0]).bit_length()
        _, j_cut = lax.fori_loop(0, n_steps, j_body,
                                 (jnp.zeros((1, LANES), jnp.int32), jnp.full((1, LANES), nch * KC, jnp.int32)))

        def drop_body(c, carry):
            off = chunk_off(c)
            blk = sc_scr[pl.ds(off, KC), :]
            drop = tie & (blk == lo) & (key_pos(off) >= j_cut)
            sc_scr[pl.ds(off, KC), :] = jnp.where(drop, -jnp.inf, blk)
            return carry

        lax.fori_loop(0, nch, drop_body, 0)

    m_scr[...] = jnp.full(m_scr.shape, NEG_BIG, jnp.float32)
    l_scr[...] = jnp.zeros(l_scr.shape, jnp.float32)
    acc_scr[...] = jnp.zeros(acc_scr.shape, jnp.float32)

    def att_body(c, carry):
        off = chunk_off(c)
        bias = jnp.where(sc_scr[pl.ds(off, KC), :] >= lo, 0.0, NEG_BIG)
        st = jnp.dot(ckv_ref[0, pl.ds(off, KC), :], qlatT_ref[0], preferred_element_type=jnp.float32)
        ps = []
        alphas = []
        for h in range(N_HEADS_A):
            sl = slice(h * LANES, (h + 1) * LANES)
            s = st[:, sl] + bias
            m_old = m_scr[:, sl]
            m_new = jnp.maximum(m_old, jnp.max(s, axis=0, keepdims=True))
            alpha = jnp.exp2(m_old - m_new)
            p = jnp.exp2(s - m_new)
            l_scr[:, sl] = alpha * l_scr[:, sl] + jnp.sum(p, axis=0, keepdims=True)
            m_scr[:, sl] = m_new
            ps.append(p.astype(MXU_DTYPE))
            alphas.append(alpha)
        p_all = jnp.concatenate(ps, axis=1)
        alpha_all = jnp.concatenate(alphas, axis=1)
        pv = jnp.dot(ckvT_ref[:, pl.ds(off, KC)], p_all, preferred_element_type=jnp.float32)
        acc_scr[...] = acc_scr[...] * alpha_all + pv
        return carry

    lax.fori_loop(0, nch, att_body, 0)

    o_t = (acc_scr[...] * (1.0 / l_scr[...])).astype(MXU_DTYPE)
    ys = []
    for h in range(N_HEADS_A):
        ys.append(jnp.dot(wuvT_ref[h], o_t[:, h * LANES:(h + 1) * LANES],
                          preferred_element_type=jnp.float32))
    ya_ref[0] = jnp.concatenate(ys, axis=0).T.astype(ya_ref.dtype)


def _dsa_call(ik, ckv, ckvT, qlatT, iqT, wT, wuvT, b, s):
    nb = s // Q_BLOCK
    hq = N_HEADS_A * LANES
    topk = min(TOPK_MAX, s // 4)
    return pl.pallas_call(
        functools.partial(_dsa_kernel, topk=topk),
        out_shape=jax.ShapeDtypeStruct((b, s, D_A), MXU_DTYPE),
        grid=(b, nb),
        in_specs=[
            pl.BlockSpec((1, s, LANES), lambda bi, qi: (bi, 0, 0)),
            pl.BlockSpec((1, s, KV_RANK), lambda bi, qi: (bi, 0, 0)),
            pl.BlockSpec((KV_RANK, s), lambda bi, qi: (0, bi)),
            pl.BlockSpec((1, KV_RANK, hq), lambda bi, qi: (bi * nb + qi, 0, 0)),
            pl.BlockSpec((1, LANES, hq), lambda bi, qi: (bi * nb + qi, 0, 0)),
            pl.BlockSpec((1, N_IDX_HEADS, LANES), lambda bi, qi: (bi * nb + qi, 0, 0)),
            pl.BlockSpec((N_HEADS_A, HEAD_DIM_A, KV_RANK), lambda bi, qi: (0, 0, 0)),
        ],
        out_specs=pl.BlockSpec((1, Q_BLOCK, D_A), lambda bi, qi: (bi, qi, 0)),
        scratch_shapes=[
            pltpu.VMEM((s, LANES), jnp.float32),
            pltpu.VMEM((KV_RANK, hq), jnp.float32),
            pltpu.VMEM((1, hq), jnp.float32),
            pltpu.VMEM((1, hq), jnp.float32),
        ],
        compiler_params=_cparams(2),
        name="dsa_attention",
    )(ik, ckv, ckvT, qlatT, iqT, wT, wuvT)


def _out_kernel(x_ref, ya_ref, ybc_ref, wo_ref, g_ref, b_ref, rwT_ref, rbias_ref,
                x1_ref, gatesT_ref, sel_scr, aff_scr):
    mix = jnp.dot(ya_ref[...], wo_ref[0:D_A, :], preferred_element_type=jnp.float32)
    mix = mix + jnp.dot(ybc_ref[...], wo_ref[D_A:, :], preferred_element_type=jnp.float32)
    x1 = _layer_norm(ALPHA * x_ref[...] + mix, g_ref[...], b_ref[...])
    x1_ref[...] = x1

    logits_t = lax.dot_general(rwT_ref[...], x1, (((1,), (1,)), ((), ())),
                               precision=lax.Precision.HIGHEST,
                               preferred_element_type=jnp.float32)
    aff = _sigmoid(logits_t)
    aff_scr[...] = aff
    sel_scr[...] = aff + rbias_ref[...]

    def row(ref, e):
        return ref[e:e + 1, :]

    grp_scores = []
    for g in range(N_EXPERT_GROUPS):
        r = [row(sel_scr, g * EXPERTS_PER_GROUP + k) for k in range(EXPERTS_PER_GROUP)]
        m1, n1 = jnp.maximum(r[0], r[1]), jnp.minimum(r[0], r[1])
        m2, n2 = jnp.maximum(r[2], r[3]), jnp.minimum(r[2], r[3])
        grp_scores.append(jnp.maximum(m1, m2) + jnp.maximum(jnp.minimum(m1, m2), jnp.maximum(n1, n2)))
    best = jnp.zeros_like(grp_scores[0], dtype=jnp.int32)
    cur = grp_scores[0]
    for g in range(1, N_EXPERT_GROUPS):
        better = grp_scores[g] > cur
        best = jnp.where(better, g, best)
        cur = jnp.where(better, grp_scores[g], cur)

    sel_aff = []
    for e in range(N_EXPERTS):
        g = e // EXPERTS_PER_GROUP
        se = row(sel_scr, e)
        rank = jnp.zeros_like(best)
        for f in range(g * EXPERTS_PER_GROUP, (g + 1) * EXPERTS_PER_GROUP):
            if f == e:
                continue
            sf = row(sel_scr, f)
            ahead = (sf > se) | ((sf == se) & (f < e))
            rank = rank + ahead.astype(jnp.int32)
        chosen = (best == g) & (rank < 2)
        sel_aff.append(jnp.where(chosen, row(aff_scr, e), 0.0))
    denom = sel_aff[0]
    for e in range(1, N_EXPERTS):
        denom = denom + sel_aff[e]
    inv = 1.0 / denom
    for e in range(N_EXPERTS):
        gatesT_ref[e:e + 1, :] = sel_aff[e] * inv


def _out_call(xf, ya, ybc, wo, g, b, rwT, rbias, tm):
    n, d = xf.shape
    full = lambda i: (0, 0)
    return pl.pallas_call(
        _out_kernel,
        out_shape=(jax.ShapeDtypeStruct((n, d), jnp.float32),
                   jax.ShapeDtypeStruct((N_EXPERTS, n), jnp.float32)),
        grid=(n // tm,),
        in_specs=[
            pl.BlockSpec((tm, d), lambda i: (i, 0)),
            pl.BlockSpec((tm, D_A), lambda i: (i, 0)),
            pl.BlockSpec((tm, D_B + D_C), lambda i: (i, 0)),
            pl.BlockSpec((d, d), full),
            pl.BlockSpec((1, d), full),
            pl.BlockSpec((1, d), full),
            pl.BlockSpec((N_EXPERTS, d), full),
            pl.BlockSpec((N_EXPERTS, 1), full),
        ],
        out_specs=(pl.BlockSpec((tm, d), lambda i: (i, 0)),
                   pl.BlockSpec((N_EXPERTS, tm), lambda i: (0, i))),
        scratch_shapes=[pltpu.VMEM((N_EXPERTS, tm), jnp.float32),
                        pltpu.VMEM((N_EXPERTS, tm), jnp.float32)],
        compiler_params=_cparams(1),
        name="out_proj_router",
    )(xf, ya, ybc, wo, g, b, rwT, rbias)


def _moe_kernel(x1_ref, gates_ref, wg_ref, wu_ref, wd_ref, g_ref, b_ref, out_ref, xb_scr, acc_scr):
    e = pl.program_id(1)

    @pl.when(e == 0)
    def _():
        xb_scr[...] = x1_ref[...].astype(xb_scr.dtype)
        acc_scr[...] = jnp.zeros(acc_scr.shape, jnp.float32)

    xb = xb_scr[...]
    hg = jnp.dot(xb, wg_ref[0], preferred_element_type=jnp.float32)
    hu = jnp.dot(xb, wu_ref[0], preferred_element_type=jnp.float32)
    gates = gates_ref[...]
    lane = lax.broadcasted_iota(jnp.int32, gates.shape, 1)
    gcol = jnp.sum(jnp.where(lane == e, gates, 0.0), axis=1, keepdims=True)
    h = (hg * _sigmoid(hg)) * hu * gcol
    acc_scr[...] += jnp.dot(h.astype(MXU_DTYPE), wd_ref[0], preferred_element_type=jnp.float32)

    @pl.when(e == pl.num_programs(1) - 1)
    def _():
        out_ref[...] = _layer_norm(ALPHA * x1_ref[...] + acc_scr[...], g_ref[...], b_ref[...])


def _moe_call(x1, gates, wg, wu, wd, g, b, tm):
    n, d = x1.shape
    ne, _, de = wg.shape
    return pl.pallas_call(
        _moe_kernel,
        out_shape=jax.ShapeDtypeStruct((n, d), jnp.float32),
        grid=(n // tm, ne),
        in_specs=[
            pl.BlockSpec((tm, d), lambda i, e: (i, 0)),
            pl.BlockSpec((tm, ne), lambda i, e: (i, 0)),
            pl.BlockSpec((1, d, de), lambda i, e: (e, 0, 0)),
            pl.BlockSpec((1, d, de), lambda i, e: (e, 0, 0)),
            pl.BlockSpec((1, de, d), lambda i, e: (e, 0, 0)),
            pl.BlockSpec((1, d), lambda i, e: (0, 0)),
            pl.BlockSpec((1, d), lambda i, e: (0, 0)),
        ],
        out_specs=pl.BlockSpec((tm, d), lambda i, e: (i, 0)),
        scratch_shapes=[pltpu.VMEM((tm, d), MXU_DTYPE), pltpu.VMEM((tm, d), jnp.float32)],
        compiler_params=_cparams(2),
        name="moe_ffn",
    )(x1, gates, wg, wu, wd, g, b)


def _arrange_w_in(w):
    d = w.shape[0]
    z = lambda k: jnp.zeros((d, k), w.dtype)
    idx_block = jnp.concatenate([w[:, _R_IK:_R_IW], z(IW_LANE - IDX_DIM), w[:, _R_IW:_R_POOL],
                                 z(LANES - IW_LANE - N_IDX_HEADS)], axis=1)
    return jnp.concatenate([w[:, _R_Q:_R_KV], w[:, _R_KV:_R_IQ], idx_block, w[:, _R_IQ:_R_IK],
                            w[:, _R_CONV:_R_END], w[:, _R_POOL:_R_CONV]], axis=1)


def _block_diag(w_pool):
    g, c, _ = w_pool.shape
    out = jnp.zeros((g * c, g * c), w_pool.dtype)
    for gi in range(g):
        out = out.at[gi * c:(gi + 1) * c, gi * c:(gi + 1) * c].set(w_pool[gi])
    return out


def _pick(n, pref):
    t = pref
    while n % t:
        t //= 2
    return t


def kernel(x, w_in, kv_norm_g, w_uk, w_uv, w_pool, pool_scale, conv_w, conv_b, conv_ln_g,
           conv_ln_b, w_o, ln1_g, ln1_b, router_w, router_bias, w_gate, w_up, w_down,
           ln2_g, ln2_b):
    b, s, d = x.shape
    n = b * s
    depth = w_in.shape[0]
    assert s % KC == 0 and d == 1024
    xf = x.reshape(n, d)
    rwT = router_w.T
    rbias = router_bias.reshape(N_EXPERTS, 1)
    tm_in = _pick(n, 512)
    ts_bc = _pick(s, 512)
    tm_out = _pick(n, 512)
    tm_moe = _pick(n, 1024)
    for l in range(depth):
        w_arr = _arrange_w_in(w_in[l]).astype(MXU_DTYPE)
        ubc, ckv, ckvT, ik, qlatT, iqT, wT = _in_call(
            xf, w_arr, kv_norm_g[l].reshape(1, KV_RANK), w_uk[l].astype(MXU_DTYPE), tm_in)
        convw = jnp.concatenate([conv_w[l].reshape(CONV_WIDTH, D_C),
                                 jnp.zeros((HALO - CONV_WIDTH, D_C), conv_w.dtype)], axis=0)
        ybc = _bc_call(ubc.reshape(b, s, C_END - C_UC), _block_diag(w_pool[l]).astype(MXU_DTYPE),
                       pool_scale[l].reshape(1, D_B), convw, conv_b[l].reshape(1, D_C),
                       conv_ln_g[l].reshape(1, D_C), conv_ln_b[l].reshape(1, D_C), ts_bc)
        wuvT = jnp.swapaxes(w_uv[l], 1, 2).astype(MXU_DTYPE)
        ya = _dsa_call(ik.reshape(b, s, LANES), ckv.reshape(b, s, KV_RANK), ckvT, qlatT, iqT, wT, wuvT, b, s)
        x1, gates_t = _out_call(xf, ya.reshape(n, D_A), ybc.reshape(n, D_B + D_C), w_o[l].astype(MXU_DTYPE),
                                ln1_g[l].reshape(1, d), ln1_b[l].reshape(1, d), rwT, rbias, tm_out)
        xf = _moe_call(x1, gates_t.T, w_gate[l].astype(MXU_DTYPE), w_up[l].astype(MXU_DTYPE),
                       w_down[l].astype(MXU_DTYPE), ln2_g[l].reshape(1, d), ln2_b[l].reshape(1, d), tm_moe)
    return xf.reshape(b, s, d)
```

```python
import functools
import math

import jax
import jax.numpy as jnp
from jax import lax
from jax.experimental import pallas as pl
from jax.experimental.pallas import tpu as pltpu

N_HEADS_A = 8
HEAD_DIM_A = 64
D_A = N_HEADS_A * HEAD_DIM_A
KV_RANK = 128
N_IDX_HEADS = 8
IDX_DIM = 32
TOPK_MAX = 256
Q_BLOCK = 128
N_POOL_GROUPS = 4
POOL_GROUP_DIM = 64
D_B = N_POOL_GROUPS * POOL_GROUP_DIM
POOL_WINDOWS = (2, 4, 8, 16)
D_C = 256
CONV_WIDTH = 31
N_EXPERTS = 16
N_EXPERT_GROUPS = 4
EXPERTS_PER_GROUP = N_EXPERTS // N_EXPERT_GROUPS
D_EXPERT = 512
DEPTH = 2
ALPHA = (2 * DEPTH) ** 0.25
LN_EPS = 1e-5

_R_Q = 0
_R_KV = _R_Q + D_A
_R_IQ = _R_KV + KV_RANK
_R_IK = _R_IQ + N_IDX_HEADS * IDX_DIM
_R_IW = _R_IK + IDX_DIM
_R_POOL = _R_IW + N_IDX_HEADS
_R_CONV = _R_POOL + D_B
_R_END = _R_CONV + 2 * D_C

LANES = 128
C_Q = 0
C_KV = C_Q + D_A
C_IDX = C_KV + KV_RANK
C_IQ = C_IDX + LANES
C_UC = C_IQ + N_IDX_HEADS * IDX_DIM
C_UP = C_UC + 2 * D_C
C_END = C_UP + D_B
IW_LANE = 96

MXU_DTYPE = jnp.bfloat16
VMEM_LIMIT = 56 * 1024 * 1024
NEG_BIG = -1e30
LOG2E = 1.4426950408889634


def _cparams(n_axes):
    return pltpu.CompilerParams(dimension_semantics=("arbitrary",) * n_axes,
                                vmem_limit_bytes=VMEM_LIMIT)


def _layer_norm(z, g, b):
    mu = jnp.mean(z, axis=-1, keepdims=True)
    zc = z - mu
    var = jnp.mean(zc * zc, axis=-1, keepdims=True)
    return zc * lax.rsqrt(var + LN_EPS) * g + b


def _sigmoid(v):
    return 1.0 / (1.0 + jnp.exp(-v))


def _in_kernel(x_ref, w_ref, g_ref, wuk_ref,
               ubc_ref, ckv_ref, ckvT_ref, ik_ref, qlatT_ref, iqT_ref, wT_ref, p_scr):
    tm = x_ref.shape[0]
    p_scr[...] = jnp.dot(x_ref[...].astype(MXU_DTYPE), w_ref[...], preferred_element_type=jnp.float32)

    ubc_ref[...] = p_scr[:, C_UC:C_END]

    c = p_scr[:, C_KV:C_KV + KV_RANK]
    ms = jnp.mean(c * c, axis=-1, keepdims=True)
    cn = c * lax.rsqrt(ms + LN_EPS) * g_ref[...]
    ckv_ref[...] = cn.astype(ckv_ref.dtype)
    ckvT_ref[...] = cn.T.astype(ckvT_ref.dtype)

    idx = p_scr[:, C_IDX:C_IDX + LANES]
    lane = lax.broadcasted_iota(jnp.int32, idx.shape, 1)
    ik_ref[...] = jnp.where(lane < IDX_DIM, idx, 0.0).astype(ik_ref.dtype)

    q_scale = HEAD_DIM_A ** -0.5 * LOG2E
    for j in range(tm // Q_BLOCK):
        r0 = j * Q_BLOCK
        iqb_t = (p_scr[r0:r0 + Q_BLOCK, C_IQ:C_IQ + N_IDX_HEADS * IDX_DIM] * (IDX_DIM ** -0.5)).T
        for h in range(N_IDX_HEADS):
            iqT_ref[j, 0:IDX_DIM, h * LANES:(h + 1) * LANES] = (
                iqb_t[h * IDX_DIM:(h + 1) * IDX_DIM, :].astype(iqT_ref.dtype))
        iqT_ref[j, IDX_DIM:LANES, :] = jnp.zeros((LANES - IDX_DIM, N_IDX_HEADS * LANES), iqT_ref.dtype)
        idx_t = p_scr[r0:r0 + Q_BLOCK, C_IDX:C_IDX + LANES].T
        wT_ref[j] = idx_t[IW_LANE:IW_LANE + N_IDX_HEADS, :] * (N_IDX_HEADS ** -0.5)
        for h in range(N_HEADS_A):
            qh = p_scr[r0:r0 + Q_BLOCK, C_Q + h * HEAD_DIM_A:C_Q + (h + 1) * HEAD_DIM_A].astype(MXU_DTYPE)
            t = lax.dot_general(wuk_ref[h], qh, (((1,), (1,)), ((), ())),
                                preferred_element_type=jnp.float32)
            qlatT_ref[j, :, h * LANES:(h + 1) * LANES] = (t * q_scale).astype(qlatT_ref.dtype)


def _in_call(xf, w_arr, kv_g, w_uk, tm):
    n, d = xf.shape
    nqb = n // Q_BLOCK
    jb = tm // Q_BLOCK
    hq = N_HEADS_A * LANES
    out_shape = (
        jax.ShapeDtypeStruct((n, C_END - C_UC), jnp.float32),
        jax.ShapeDtypeStruct((n, KV_RANK), MXU_DTYPE),
        jax.ShapeDtypeStruct((KV_RANK, n), MXU_DTYPE),
        jax.ShapeDtypeStruct((n, LANES), MXU_DTYPE),
        jax.ShapeDtypeStruct((nqb, KV_RANK, hq), MXU_DTYPE),
        jax.ShapeDtypeStruct((nqb, LANES, hq), MXU_DTYPE),
        jax.ShapeDtypeStruct((nqb, N_IDX_HEADS, LANES), jnp.float32),
    )
    return pl.pallas_call(
        _in_kernel,
        out_shape=out_shape,
        grid=(n // tm,),
        in_specs=[
            pl.BlockSpec((tm, d), lambda i: (i, 0)),
            pl.BlockSpec((d, C_END), lambda i: (0, 0)),
            pl.BlockSpec((1, KV_RANK), lambda i: (0, 0)),
            pl.BlockSpec((N_HEADS_A, KV_RANK, HEAD_DIM_A), lambda i: (0, 0, 0)),
        ],
        out_specs=(
            pl.BlockSpec((tm, C_END - C_UC), lambda i: (i, 0)),
            pl.BlockSpec((tm, KV_RANK), lambda i: (i, 0)),
            pl.BlockSpec((KV_RANK, tm), lambda i: (0, i)),
            pl.BlockSpec((tm, LANES), lambda i: (i, 0)),
            pl.BlockSpec((jb, KV_RANK, hq), lambda i: (i, 0, 0)),
            pl.BlockSpec((jb, LANES, hq), lambda i: (i, 0, 0)),
            pl.BlockSpec((jb, N_IDX_HEADS, LANES), lambda i: (i, 0, 0)),
        ),
        scratch_shapes=[pltpu.VMEM((tm, C_END), jnp.float32)],
        compiler_params=_cparams(1),
        name="in_proj",
    )(xf, w_arr, kv_g, w_uk)


HALO = 32


def _bc_kernel(ubc_ref, wpool_ref, pscale_ref, convw_ref, convb_ref, lng_ref, lnb_ref,
               out_ref, ext_c, ext_p):
    t_idx = pl.program_id(1)
    ts = ubc_ref.shape[1]

    @pl.when(t_idx == 0)
    def _():
        ext_c[0:HALO, :] = jnp.zeros((HALO, D_C), jnp.float32)
        ext_p[0:HALO, :] = jnp.zeros((HALO, D_B), jnp.float32)

    a = ubc_ref[0, :, 0:D_C]
    gate = ubc_ref[0, :, D_C:2 * D_C]
    up = ubc_ref[0, :, 2 * D_C:2 * D_C + D_B]
    ext_c[HALO:HALO + ts, :] = a * _sigmoid(gate)
    ext_p[HALO:HALO + ts, :] = up

    base = HALO - (CONV_WIDTH - 1)
    conv = jnp.zeros((ts, D_C), jnp.float32) + convb_ref[...]
    for k in range(CONV_WIDTH):
        conv = conv + ext_c[base + k:base + k + ts, :] * convw_ref[k:k + 1, :]
    hc = _layer_norm(conv, lng_ref[...], lnb_ref[...])
    yc = hc * _sigmoid(hc)

    def shifted(j):
        return ext_p[HALO - j:HALO - j + ts, :]
    s2 = shifted(0) + shifted(1)
    s4 = s2 + (shifted(2) + shifted(3))
    s8 = s4 + ((shifted(4) + shifted(5)) + (shifted(6) + shifted(7)))
    s16 = s8 + (((shifted(8) + shifted(9)) + (shifted(10) + shifted(11)))
                + ((shifted(12) + shifted(13)) + (shifted(14) + shifted(15))))
    grp = lax.broadcasted_iota(jnp.int32, (ts, D_B), 1) // POOL_GROUP_DIM
    ssel = jnp.where(grp == 0, s2, jnp.where(grp == 1, s4, jnp.where(grp == 2, s8, s16)))
    wlane = jnp.where(grp == 0, 2.0, jnp.where(grp == 1, 4.0, jnp.where(grp == 2, 8.0, 16.0)))
    tpos = (t_idx * ts + lax.broadcasted_iota(jnp.int32, (ts, D_B), 0) + 1).astype(jnp.float32)
    cnt = jnp.minimum(tpos, wlane)
    pooled = ssel / cnt - up
    yb = jnp.dot(pooled.astype(MXU_DTYPE), wpool_ref[...], preferred_element_type=jnp.float32) * pscale_ref[...]

    out_ref[0, :, 0:D_B] = yb.astype(out_ref.dtype)
    out_ref[0, :, D_B:D_B + D_C] = yc.astype(out_ref.dtype)

    ext_c[0:HALO, :] = ext_c[ts:ts + HALO, :]
    ext_p[0:HALO, :] = ext_p[ts:ts + HALO, :]


def _bc_call(ubc, wpool_bd, pscale, convw, convb, lng, lnb, ts):
    b, s, _ = ubc.shape
    full2 = lambda bi, ti: (0, 0)
    return pl.pallas_call(
        _bc_kernel,
        out_shape=jax.ShapeDtypeStruct((b, s, D_B + D_C), MXU_DTYPE),
        grid=(b, s // ts),
        in_specs=[
            pl.BlockSpec((1, ts, C_END - C_UC), lambda bi, ti: (bi, ti, 0)),
            pl.BlockSpec((D_B, D_B), full2),
            pl.BlockSpec((1, D_B), full2),
            pl.BlockSpec((HALO, D_C), full2),
            pl.BlockSpec((1, D_C), full2),
            pl.BlockSpec((1, D_C), full2),
            pl.BlockSpec((1, D_C), full2),
        ],
        out_specs=pl.BlockSpec((1, ts, D_B + D_C), lambda bi, ti: (bi, ti, 0)),
        scratch_shapes=[pltpu.VMEM((ts + HALO, D_C), jnp.float32),
                        pltpu.VMEM((ts + HALO, D_B), jnp.float32)],
        compiler_params=_cparams(2),
        name="pool_conv",
    )(ubc, wpool_bd, pscale, convw, convb, lng, lnb)


KC = 1024
ONES_ROWS = 16
FOLD_ROWS = 64
SEL_BASE_PASSES = 12


def _dsa_kernel(ik_ref, ckv_ref, ckvT_ref, qlatT_ref, iqT_ref, wT_ref, wuvT_ref,
                ya_ref, sc_scr, acc_scr, m_scr, *, topk):
    i = pl.program_id(1)
    qpb = KC // Q_BLOCK
    nch = (i + qpb) // qpb
    q_pos = i * Q_BLOCK + lax.broadcasted_iota(jnp.int32, (1, LANES), 1)
    n_causal = q_pos + 1
    target = jnp.minimum(topk, n_causal)

    def fold(v, op):
        return op(v.reshape(KC // FOLD_ROWS, FOLD_ROWS, LANES), axis=0)

    def chunk_off(c):
        return pl.multiple_of(c * KC, KC)

    def key_pos(off):
        return off + lax.broadcasted_iota(jnp.int32, (KC, LANES), 0)

    def idx_chunk(c, stats, masked):
        off = chunk_off(c)
        lt = jnp.dot(ik_ref[0, pl.ds(off, KC), :], iqT_ref[0], preferred_element_type=jnp.float32)
        sc = jnp.zeros((KC, LANES), jnp.float32)
        for h in range(N_IDX_HEADS):
            sc = sc + jnp.maximum(lt[:, h * LANES:(h + 1) * LANES], 0.0) * wT_ref[0, h:h + 1, :]
        sc_for_min = sc
        if masked:
            causal = key_pos(off) <= q_pos
            sc_for_min = jnp.where(causal, sc, jnp.inf)
            sc = jnp.where(causal, sc, -jnp.inf)
        sc_scr[pl.ds(off, KC), :] = sc
        mx, mn, ge0, gt0 = stats
        return (jnp.maximum(mx, fold(sc, jnp.max)),
                jnp.minimum(mn, fold(sc_for_min, jnp.min)),
                ge0 + fold(jnp.where(sc >= 0.0, 1, 0).astype(jnp.int32), jnp.sum),
                gt0 + fold(jnp.where(sc > 0.0, 1, 0).astype(jnp.int32), jnp.sum))

    stats = (jnp.full((FOLD_ROWS, LANES), -jnp.inf, jnp.float32), jnp.full((FOLD_ROWS, LANES), jnp.inf, jnp.float32),
             jnp.zeros((FOLD_ROWS, LANES), jnp.int32), jnp.zeros((FOLD_ROWS, LANES), jnp.int32))
    stats = lax.fori_loop(0, nch - 1, lambda c, st: idx_chunk(c, st, False), stats)
    stats = idx_chunk(nch - 1, stats, True)
    rmax = jnp.max(stats[0], axis=0, keepdims=True)
    rmin = jnp.min(stats[1], axis=0, keepdims=True)
    c_ge0 = jnp.sum(stats[2], axis=0, keepdims=True)
    c_gt0 = jnp.sum(stats[3], axis=0, keepdims=True)

    def count_where(pred):
        def body(c, acc):
            off = chunk_off(c)
            ind = jnp.where(pred(sc_scr[pl.ds(off, KC), :], off), 1, 0).astype(jnp.int32)
            return acc + fold(ind, jnp.sum)
        acc = lax.fori_loop(0, nch, body, jnp.zeros((FOLD_ROWS, LANES), jnp.int32))
        return jnp.sum(acc, axis=0, keepdims=True)

    def unfinished(c_lo, fin):
        return (c_lo != target) & (fin == 0)

    def sel_pass(state):
        lo, hi, c_lo, fin = state
        mid = 0.5 * lo + 0.5 * hi
        mid = jnp.where(mid <= lo, hi, mid)
        active = unfinished(c_lo, fin)
        c = count_where(lambda blk, off: blk >= mid)
        ge = c >= target
        up = active & ge
        dn = active & jnp.logical_not(ge)
        fin = jnp.where(active & (mid == hi), 1, fin)
        return jnp.where(up, mid, lo), jnp.where(dn, mid, hi), jnp.where(up, c, c_lo), fin

    def any_lane(mask):
        return jnp.max(mask.astype(jnp.int32))

    at_zero = (c_gt0 < target) & (target <= c_ge0)
    above = c_gt0 >= target
    zero = jnp.zeros((1, LANES), jnp.float32)
    lo0 = jnp.where(at_zero | above, zero, rmin)
    hi0 = jnp.where(above, rmax, zero)
    c_lo0 = jnp.where(at_zero | above, c_ge0, n_causal)
    state0 = (lo0, hi0, c_lo0, at_zero.astype(jnp.int32))
    go0 = any_lane(unfinished(state0[2], state0[3]))
    n_blocks = i + 1
    n_fixed = SEL_BASE_PASSES + sum((n_blocks >= (1 << k)).astype(jnp.int32) for k in range(1, 16))
    state = lax.cond(go0 > 0,
                     lambda st: lax.fori_loop(0, n_fixed, lambda _, s_: sel_pass(s_), st),
                     lambda st: st, state0)

    def sel_more(carry):
        st = sel_pass(carry[0])
        return st, any_lane(unfinished(st[2], st[3]))

    state, _ = lax.while_loop(lambda carry: carry[1] > 0, sel_more,
                              (state, any_lane(unfinished(state[2], state[3]))))
    lo, _, c_lo, _ = state

    tie = c_lo > target

    @pl.when(any_lane(tie) > 0)
    def _():
        need = target - count_where(lambda blk, off: blk > lo)

        def j_body(_, st):
            j_lo, j_hi = st
            j_mid = (j_lo + j_hi) >> 1
            f = count_where(lambda blk, off: (blk == lo) & (key_pos(off) < j_mid))
            ok = f >= need
            return jnp.where(ok, j_lo, j_mid), jnp.where(ok, j_mid, j_hi)

        n_steps = int(sc_scr.shape[0]).bit_length()
        _, j_cut = lax.fori_loop(0, n_steps, j_body,
                                 (jnp.zeros((1, LANES), jnp.int32), jnp.full((1, LANES), nch * KC, jnp.int32)))

        def drop_body(c, carry):
            off = chunk_off(c)
            blk = sc_scr[pl.ds(off, KC), :]
            drop = tie & (blk == lo) & (key_pos(off) >= j_cut)
            sc_scr[pl.ds(off, KC), :] = jnp.where(drop, -jnp.inf, blk)
            return carry

        lax.fori_loop(0, nch, drop_body, 0)

    hq = N_HEADS_A * LANES
    eye_rep = jnp.where(lax.broadcasted_iota(jnp.int32, (LANES, hq), 0)
                        == lax.broadcasted_iota(jnp.int32, (LANES, hq), 1) % LANES, 1.0, 0.0).astype(MXU_DTYPE)
    q_aug = jnp.concatenate([qlatT_ref[0], eye_rep], axis=0)
    ones_rows = jnp.ones((ONES_ROWS, KC), MXU_DTYPE)
    m_scr[...] = jnp.full(m_scr.shape, NEG_BIG, jnp.float32)
    acc_scr[...] = jnp.zeros(acc_scr.shape, jnp.float32)

    def att_body(c, carry):
        off = chunk_off(c)
        bias = jnp.where(sc_scr[pl.ds(off, KC), :] >= lo, 0.0, NEG_BIG).astype(MXU_DTYPE)
        st = jnp.dot(jnp.concatenate([ckv_ref[0, pl.ds(off, KC), :], bias], axis=1), q_aug,
                     preferred_element_type=jnp.float32)
        ps = []
        alphas = []
        for h in range(N_HEADS_A):
            sl = slice(h * LANES, (h + 1) * LANES)
            s = st[:, sl]
            m_old = m_scr[:, sl]
            m_new = jnp.maximum(m_old, jnp.max(s, axis=0, keepdims=True))
            m_scr[:, sl] = m_new
            ps.append(jnp.exp2(s - m_new).astype(MXU_DTYPE))
            alphas.append(jnp.exp2(m_old - m_new))
        p_all = jnp.concatenate(ps, axis=1)
        alpha_all = jnp.concatenate(alphas, axis=1)
        v_aug = jnp.concatenate([ckvT_ref[:, pl.ds(off, KC)], ones_rows], axis=0)
        acc_scr[...] = acc_scr[...] * alpha_all + jnp.dot(v_aug, p_all, preferred_element_type=jnp.float32)
        return carry

    lax.fori_loop(0, nch, att_body, 0)

    denom = acc_scr[KV_RANK:KV_RANK + 1, :]
    o_t = (acc_scr[0:KV_RANK, :] * (1.0 / denom)).astype(MXU_DTYPE)
    ys = []
    for h in range(N_HEADS_A):
        ys.append(jnp.dot(wuvT_ref[h], o_t[:, h * LANES:(h + 1) * LANES],
                          preferred_element_type=jnp.float32))
    ya_ref[0] = jnp.concatenate(ys, axis=0).T.astype(ya_ref.dtype)


def _dsa_call(ik, ckv, ckvT, qlatT, iqT, wT, wuvT, b, s):
    nb = s // Q_BLOCK
    hq = N_HEADS_A * LANES
    topk = min(TOPK_MAX, s // 4)
    return pl.pallas_call(
        functools.partial(_dsa_kernel, topk=topk),
        out_shape=jax.ShapeDtypeStruct((b, s, D_A), MXU_DTYPE),
        grid=(b, nb),
        in_specs=[
            pl.BlockSpec((1, s, LANES), lambda bi, qi: (bi, 0, 0)),
            pl.BlockSpec((1, s, KV_RANK), lambda bi, qi: (bi, 0, 0)),
            pl.BlockSpec((KV_RANK, s), lambda bi, qi: (0, bi)),
            pl.BlockSpec((1, KV_RANK, hq), lambda bi, qi: (bi * nb + qi, 0, 0)),
            pl.BlockSpec((1, LANES, hq), lambda bi, qi: (bi * nb + qi, 0, 0)),
            pl.BlockSpec((1, N_IDX_HEADS, LANES), lambda bi, qi: (bi * nb + qi, 0, 0)),
            pl.BlockSpec((N_HEADS_A, HEAD_DIM_A, KV_RANK), lambda bi, qi: (0, 0, 0)),
        ],
        out_specs=pl.BlockSpec((1, Q_BLOCK, D_A), lambda bi, qi: (bi, qi, 0)),
        scratch_shapes=[
            pltpu.VMEM((s, LANES), jnp.float32),
            pltpu.VMEM((KV_RANK + ONES_ROWS, hq), jnp.float32),
            pltpu.VMEM((1, hq), jnp.float32),
        ],
        compiler_params=_cparams(2),
        name="dsa_attention",
    )(ik, ckv, ckvT, qlatT, iqT, wT, wuvT)


def _out_kernel(x_ref, ya_ref, ybc_ref, wo_ref, g_ref, b_ref, rwT_ref, rbias_ref,
                x1_ref, gatesT_ref, sel_scr, aff_scr):
    mix = jnp.dot(ya_ref[...], wo_ref[0:D_A, :], preferred_element_type=jnp.float32)
    mix = mix + jnp.dot(ybc_ref[...], wo_ref[D_A:, :], preferred_element_type=jnp.float32)
    x1 = _layer_norm(ALPHA * x_ref[...] + mix, g_ref[...], b_ref[...])
    x1_ref[...] = x1

    logits_t = lax.dot_general(rwT_ref[...], x1, (((1,), (1,)), ((), ())),
                               precision=lax.Precision.HIGHEST,
                               preferred_element_type=jnp.float32)
    aff = _sigmoid(logits_t)
    aff_scr[...] = aff
    sel_scr[...] = aff + rbias_ref[...]

    def row(ref, e):
        return ref[e:e + 1, :]

    grp_scores = []
    for g in range(N_EXPERT_GROUPS):
        r = [row(sel_scr, g * EXPERTS_PER_GROUP + k) for k in range(EXPERTS_PER_GROUP)]
        m1, n1 = jnp.maximum(r[0], r[1]), jnp.minimum(r[0], r[1])
        m2, n2 = jnp.maximum(r[2], r[3]), jnp.minimum(r[2], r[3])
        grp_scores.append(jnp.maximum(m1, m2) + jnp.maximum(jnp.minimum(m1, m2), jnp.maximum(n1, n2)))
    best = jnp.zeros_like(grp_scores[0], dtype=jnp.int32)
    cur = grp_scores[0]
    for g in range(1, N_EXPERT_GROUPS):
        better = grp_scores[g] > cur
        best = jnp.where(better, g, best)
        cur = jnp.where(better, grp_scores[g], cur)

    sel_aff = []
    for e in range(N_EXPERTS):
        g = e // EXPERTS_PER_GROUP
        se = row(sel_scr, e)
        rank = jnp.zeros_like(best)
        for f in range(g * EXPERTS_PER_GROUP, (g + 1) * EXPERTS_PER_GROUP):
            if f == e:
                continue
            sf = row(sel_scr, f)
            ahead = (sf > se) | ((sf == se) & (f < e))
            rank = rank + ahead.astype(jnp.int32)
        chosen = (best == g) & (rank < 2)
        sel_aff.append(jnp.where(chosen, row(aff_scr, e), 0.0))
    denom = sel_aff[0]
    for e in range(1, N_EXPERTS):
        denom = denom + sel_aff[e]
    inv = 1.0 / denom
    for e in range(N_EXPERTS):
        gatesT_ref[e:e + 1, :] = sel_aff[e] * inv


def _out_call(xf, ya, ybc, wo, g, b, rwT, rbias, tm):
    n, d = xf.shape
    full = lambda i: (0, 0)
    return pl.pallas_call(
        _out_kernel,
        out_shape=(jax.ShapeDtypeStruct((n, d), jnp.float32),
                   jax.ShapeDtypeStruct((N_EXPERTS, n), jnp.float32)),
        grid=(n // tm,),
        in_specs=[
            pl.BlockSpec((tm, d), lambda i: (i, 0)),
            pl.BlockSpec((tm, D_A), lambda i: (i, 0)),
            pl.BlockSpec((tm, D_B + D_C), lambda i: (i, 0)),
            pl.BlockSpec((d, d), full),
            pl.BlockSpec((1, d), full),
            pl.BlockSpec((1, d), full),
            pl.BlockSpec((N_EXPERTS, d), full),
            pl.BlockSpec((N_EXPERTS, 1), full),
        ],
        out_specs=(pl.BlockSpec((tm, d), lambda i: (i, 0)),
                   pl.BlockSpec((N_EXPERTS, tm), lambda i: (0, i))),
        scratch_shapes=[pltpu.VMEM((N_EXPERTS, tm), jnp.float32),
                        pltpu.VMEM((N_EXPERTS, tm), jnp.float32)],
        compiler_params=_cparams(1),
        name="out_proj_router",
    )(xf, ya, ybc, wo, g, b, rwT, rbias)


MOE_CAP = 160


def _moe_kernel(x1_ref, gates_t_ref, tri_ref, wg_ref, wu_ref, wd_ref, g_ref, b_ref,
                out_ref, xb_scr, acc_scr, rank_scr):
    e = pl.program_id(1)
    tm = x1_ref.shape[0]

    @pl.when(e == 0)
    def _():
        xb_scr[...] = x1_ref[...].astype(xb_scr.dtype)
        acc_scr[...] = jnp.zeros(acc_scr.shape, jnp.float32)
        routed_all = jnp.where(gates_t_ref[...] != 0.0, 1.0, 0.0).astype(MXU_DTYPE)
        rank_scr[...] = jnp.dot(routed_all, tri_ref[...], preferred_element_type=jnp.float32)

    g_row = gates_t_ref[pl.ds(e, 1), :]
    routed = g_row != 0.0
    slot = jnp.where(routed, rank_scr[pl.ds(e, 1), :].astype(jnp.int32), -1)
    n_routed = jnp.sum(routed.astype(jnp.int32))

    def pass_body(b, carry):
        row = b * MOE_CAP + lax.broadcasted_iota(jnp.int32, (MOE_CAP, tm), 0)
        hit = row == slot
        onehot = jnp.where(hit, 1.0, 0.0).astype(MXU_DTYPE)
        gate = jnp.sum(jnp.where(hit, g_row, 0.0), axis=1, keepdims=True)
        xe = jnp.dot(onehot, xb_scr[...], preferred_element_type=jnp.float32).astype(MXU_DTYPE)
        hg = jnp.dot(xe, wg_ref[0], preferred_element_type=jnp.float32)
        hu = jnp.dot(xe, wu_ref[0], preferred_element_type=jnp.float32)
        h = (hg * _sigmoid(hg)) * hu * gate
        y = jnp.dot(h.astype(MXU_DTYPE), wd_ref[0], preferred_element_type=jnp.float32)
        acc_scr[...] += lax.dot_general(onehot, y.astype(MXU_DTYPE), (((0,), (0,)), ((), ())),
                                        preferred_element_type=jnp.float32)
        return carry

    lax.fori_loop(0, (n_routed + MOE_CAP - 1) // MOE_CAP, pass_body, 0)

    @pl.when(e == pl.num_programs(1) - 1)
    def _():
        out_ref[...] = _layer_norm(ALPHA * x1_ref[...] + acc_scr[...], g_ref[...], b_ref[...])


def _moe_call(x1, gates_t, wg, wu, wd, g, b, tm):
    n, d = x1.shape
    ne, _, de = wg.shape
    tri = jnp.triu(jnp.ones((tm, tm), jnp.float32), k=1).astype(MXU_DTYPE)
    return pl.pallas_call(
        _moe_kernel,
        out_shape=jax.ShapeDtypeStruct((n, d), jnp.float32),
        grid=(n // tm, ne),
        in_specs=[
            pl.BlockSpec((tm, d), lambda i, e: (i, 0)),
            pl.BlockSpec((ne, tm), lambda i, e: (0, i)),
            pl.BlockSpec((tm, tm), lambda i, e: (0, 0)),
            pl.BlockSpec((1, d, de), lambda i, e: (e, 0, 0)),
            pl.BlockSpec((1, d, de), lambda i, e: (e, 0, 0)),
            pl.BlockSpec((1, de, d), lambda i, e: (e, 0, 0)),
            pl.BlockSpec((1, d), lambda i, e: (0, 0)),
            pl.BlockSpec((1, d), lambda i, e: (0, 0)),
        ],
        out_specs=pl.BlockSpec((tm, d), lambda i, e: (i, 0)),
        scratch_shapes=[pltpu.VMEM((tm, d), MXU_DTYPE), pltpu.VMEM((tm, d), jnp.float32),
                        pltpu.VMEM((ne, tm), jnp.float32)],
        compiler_params=_cparams(2),
        name="moe_ffn",
    )(x1, gates_t, tri, wg, wu, wd, g, b)


def _arrange_w_in(w):
    d = w.shape[0]
    z = lambda k: jnp.zeros((d, k), w.dtype)
    idx_block = jnp.concatenate([w[:, _R_IK:_R_IW], z(IW_LANE - IDX_DIM), w[:, _R_IW:_R_POOL],
                                 z(LANES - IW_LANE - N_IDX_HEADS)], axis=1)
    return jnp.concatenate([w[:, _R_Q:_R_KV], w[:, _R_KV:_R_IQ], idx_block, w[:, _R_IQ:_R_IK],
                            w[:, _R_CONV:_R_END], w[:, _R_POOL:_R_CONV]], axis=1)


def _block_diag(w_pool):
    g, c, _ = w_pool.shape
    out = jnp.zeros((g * c, g * c), w_pool.dtype)
    for gi in range(g):
        out = out.at[gi * c:(gi + 1) * c, gi * c:(gi + 1) * c].set(w_pool[gi])
    return out


def _pick(n, pref):
    t = pref
    while n % t:
        t //= 2
    return t


def kernel(x, w_in, kv_norm_g, w_uk, w_uv, w_pool, pool_scale, conv_w, conv_b, conv_ln_g,
           conv_ln_b, w_o, ln1_g, ln1_b, router_w, router_bias, w_gate, w_up, w_down,
           ln2_g, ln2_b):
    b, s, d = x.shape
    n = b * s
    depth = w_in.shape[0]
    assert s % KC == 0 and d == 1024
    xf = x.reshape(n, d)
    rwT = router_w.T
    rbias = router_bias.reshape(N_EXPERTS, 1)
    tm_in = _pick(n, 512)
    ts_bc = _pick(s, 512)
    tm_out = _pick(n, 512)
    tm_moe = _pick(n, 1024)
    for l in range(depth):
        w_arr = _arrange_w_in(w_in[l]).astype(MXU_DTYPE)
        ubc, ckv, ckvT, ik, qlatT, iqT, wT = _in_call(
            xf, w_arr, kv_norm_g[l].reshape(1, KV_RANK), w_uk[l].astype(MXU_DTYPE), tm_in)
        convw = jnp.concatenate([conv_w[l].reshape(CONV_WIDTH, D_C),
                                 jnp.zeros((HALO - CONV_WIDTH, D_C), conv_w.dtype)], axis=0)
        ybc = _bc_call(ubc.reshape(b, s, C_END - C_UC), _block_diag(w_pool[l]).astype(MXU_DTYPE),
                       pool_scale[l].reshape(1, D_B), convw, conv_b[l].reshape(1, D_C),
                       conv_ln_g[l].reshape(1, D_C), conv_ln_b[l].reshape(1, D_C), ts_bc)
        wuvT = jnp.swapaxes(w_uv[l], 1, 2).astype(MXU_DTYPE)
        ya = _dsa_call(ik.reshape(b, s, LANES), ckv.reshape(b, s, KV_RANK), ckvT, qlatT, iqT, wT, wuvT, b, s)
        x1, gates_t = _out_call(xf, ya.reshape(n, D_A), ybc.reshape(n, D_B + D_C), w_o[l].astype(MXU_DTYPE),
                                ln1_g[l].reshape(1, d), ln1_b[l].reshape(1, d), rwT, rbias, tm_out)
        xf = _moe_call(x1, gates_t, w_gate[l].astype(MXU_DTYPE), w_up[l].astype(MXU_DTYPE),
                       w_down[l].astype(MXU_DTYPE), ln2_g[l].reshape(1, d), ln2_b[l].reshape(1, d), tm_moe)
    return xf.reshape(b, s, d)
```

```python
import functools
import math

import jax
import jax.numpy as jnp
from jax import lax
from jax.experimental import pallas as pl
from jax.experimental.pallas import tpu as pltpu

N_HEADS_A = 8
HEAD_DIM_A = 64
D_A = N_HEADS_A * HEAD_DIM_A
KV_RANK = 128
N_IDX_HEADS = 8
IDX_DIM = 32
TOPK_MAX = 256
Q_BLOCK = 128
N_POOL_GROUPS = 4
POOL_GROUP_DIM = 64
D_B = N_POOL_GROUPS * POOL_GROUP_DIM
POOL_WINDOWS = (2, 4, 8, 16)
D_C = 256
CONV_WIDTH = 31
N_EXPERTS = 16
N_EXPERT_GROUPS = 4
EXPERTS_PER_GROUP = N_EXPERTS // N_EXPERT_GROUPS
D_EXPERT = 512
DEPTH = 2
ALPHA = (2 * DEPTH) ** 0.25
LN_EPS = 1e-5

_R_Q = 0
_R_KV = _R_Q + D_A
_R_IQ = _R_KV + KV_RANK
_R_IK = _R_IQ + N_IDX_HEADS * IDX_DIM
_R_IW = _R_IK + IDX_DIM
_R_POOL = _R_IW + N_IDX_HEADS
_R_CONV = _R_POOL + D_B
_R_END = _R_CONV + 2 * D_C

LANES = 128
C_Q = 0
C_KV = C_Q + D_A
C_IDX = C_KV + KV_RANK
C_IQ = C_IDX + LANES
C_UC = C_IQ + N_IDX_HEADS * IDX_DIM
C_UP = C_UC + 2 * D_C
C_END = C_UP + D_B
IW_LANE = 96

MXU_DTYPE = jnp.bfloat16
VMEM_LIMIT = 56 * 1024 * 1024
NEG_BIG = -1e30
LOG2E = 1.4426950408889634


def _cparams(n_axes):
    return pltpu.CompilerParams(dimension_semantics=("arbitrary",) * n_axes,
                                vmem_limit_bytes=VMEM_LIMIT)


def _layer_norm(z, g, b):
    mu = jnp.mean(z, axis=-1, keepdims=True)
    zc = z - mu
    var = jnp.mean(zc * zc, axis=-1, keepdims=True)
    return zc * lax.rsqrt(var + LN_EPS) * g + b


def _sigmoid(v):
    return 1.0 / (1.0 + jnp.exp(-v))


def _in_kernel(x_ref, w_ref, g_ref, wuk_ref,
               ubc_ref, ckv_ref, ckvT_ref, ik_ref, qlatT_ref, iqT_ref, wT_ref, p_scr):
    tm = x_ref.shape[0]
    p_scr[...] = jnp.dot(x_ref[...].astype(MXU_DTYPE), w_ref[...], preferred_element_type=jnp.float32)

    ubc_ref[...] = p_scr[:, C_UC:C_END]

    c = p_scr[:, C_KV:C_KV + KV_RANK]
    ms = jnp.mean(c * c, axis=-1, keepdims=True)
    cn = c * lax.rsqrt(ms + LN_EPS) * g_ref[...]
    ckv_ref[...] = cn.astype(ckv_ref.dtype)
    ckvT_ref[...] = cn.T.astype(ckvT_ref.dtype)

    idx = p_scr[:, C_IDX:C_IDX + LANES]
    lane = lax.broadcasted_iota(jnp.int32, idx.shape, 1)
    ik_ref[...] = jnp.where(lane < IDX_DIM, idx, 0.0).astype(ik_ref.dtype)

    q_scale = HEAD_DIM_A ** -0.5 * LOG2E
    for j in range(tm // Q_BLOCK):
        r0 = j * Q_BLOCK
        iqb_t = (p_scr[r0:r0 + Q_BLOCK, C_IQ:C_IQ + N_IDX_HEADS * IDX_DIM] * (IDX_DIM ** -0.5)).T
        for h in range(N_IDX_HEADS):
            iqT_ref[j, 0:IDX_DIM, h * LANES:(h + 1) * LANES] = (
                iqb_t[h * IDX_DIM:(h + 1) * IDX_DIM, :].astype(iqT_ref.dtype))
        iqT_ref[j, IDX_DIM:LANES, :] = jnp.zeros((LANES - IDX_DIM, N_IDX_HEADS * LANES), iqT_ref.dtype)
        idx_t = p_scr[r0:r0 + Q_BLOCK, C_IDX:C_IDX + LANES].T
        wT_ref[j] = idx_t[IW_LANE:IW_LANE + N_IDX_HEADS, :] * (N_IDX_HEADS ** -0.5)
        for h in range(N_HEADS_A):
            qh = p_scr[r0:r0 + Q_BLOCK, C_Q + h * HEAD_DIM_A:C_Q + (h + 1) * HEAD_DIM_A].astype(MXU_DTYPE)
            t = lax.dot_general(wuk_ref[h], qh, (((1,), (1,)), ((), ())),
                                preferred_element_type=jnp.float32)
            qlatT_ref[j, :, h * LANES:(h + 1) * LANES] = (t * q_scale).astype(qlatT_ref.dtype)


def _in_call(xf, w_arr, kv_g, w_uk, tm):
    n, d = xf.shape
    nqb = n // Q_BLOCK
    jb = tm // Q_BLOCK
    hq = N_HEADS_A * LANES
    out_shape = (
        jax.ShapeDtypeStruct((n, C_END - C_UC), jnp.float32),
        jax.ShapeDtypeStruct((n, KV_RANK), MXU_DTYPE),
        jax.ShapeDtypeStruct((KV_RANK, n), MXU_DTYPE),
        jax.ShapeDtypeStruct((n, LANES), MXU_DTYPE),
        jax.ShapeDtypeStruct((nqb, KV_RANK, hq), MXU_DTYPE),
        jax.ShapeDtypeStruct((nqb, LANES, hq), MXU_DTYPE),
        jax.ShapeDtypeStruct((nqb, N_IDX_HEADS, LANES), jnp.float32),
    )
    return pl.pallas_call(
        _in_kernel,
        out_shape=out_shape,
        grid=(n // tm,),
        in_specs=[
            pl.BlockSpec((tm, d), lambda i: (i, 0)),
            pl.BlockSpec((d, C_END), lambda i: (0, 0)),
            pl.BlockSpec((1, KV_RANK), lambda i: (0, 0)),
            pl.BlockSpec((N_HEADS_A, KV_RANK, HEAD_DIM_A), lambda i: (0, 0, 0)),
        ],
        out_specs=(
            pl.BlockSpec((tm, C_END - C_UC), lambda i: (i, 0)),
            pl.BlockSpec((tm, KV_RANK), lambda i: (i, 0)),
            pl.BlockSpec((KV_RANK, tm), lambda i: (0, i)),
            pl.BlockSpec((tm, LANES), lambda i: (i, 0)),
            pl.BlockSpec((jb, KV_RANK, hq), lambda i: (i, 0, 0)),
            pl.BlockSpec((jb, LANES, hq), lambda i: (i, 0, 0)),
            pl.BlockSpec((jb, N_IDX_HEADS, LANES), lambda i: (i, 0, 0)),
        ),
        scratch_shapes=[pltpu.VMEM((tm, C_END), jnp.float32)],
        compiler_params=_cparams(1),
        name="in_proj",
    )(xf, w_arr, kv_g, w_uk)


HALO = 32


def _bc_kernel(ubc_ref, wpool_ref, pscale_ref, convw_ref, convb_ref, lng_ref, lnb_ref,
               out_ref, ext_c, ext_p):
    t_idx = pl.program_id(1)
    ts = ubc_ref.shape[1]

    @pl.when(t_idx == 0)
    def _():
        ext_c[0:HALO, :] = jnp.zeros((HALO, D_C), jnp.float32)
        ext_p[0:HALO, :] = jnp.zeros((HALO, D_B), jnp.float32)

    a = ubc_ref[0, :, 0:D_C]
    gate = ubc_ref[0, :, D_C:2 * D_C]
    up = ubc_ref[0, :, 2 * D_C:2 * D_C + D_B]
    ext_c[HALO:HALO + ts, :] = a * _sigmoid(gate)
    ext_p[HALO:HALO + ts, :] = up

    base = HALO - (CONV_WIDTH - 1)
    conv = jnp.zeros((ts, D_C), jnp.float32) + convb_ref[...]
    for k in range(CONV_WIDTH):
        conv = conv + ext_c[base + k:base + k + ts, :] * convw_ref[k:k + 1, :]
    hc = _layer_norm(conv, lng_ref[...], lnb_ref[...])
    yc = hc * _sigmoid(hc)

    def shifted(j):
        return ext_p[HALO - j:HALO - j + ts, :]
    s2 = shifted(0) + shifted(1)
    s4 = s2 + (shifted(2) + shifted(3))
    s8 = s4 + ((shifted(4) + shifted(5)) + (shifted(6) + shifted(7)))
    s16 = s8 + (((shifted(8) + shifted(9)) + (shifted(10) + shifted(11)))
                + ((shifted(12) + shifted(13)) + (shifted(14) + shifted(15))))
    grp = lax.broadcasted_iota(jnp.int32, (ts, D_B), 1) // POOL_GROUP_DIM
    ssel = jnp.where(grp == 0, s2, jnp.where(grp == 1, s4, jnp.where(grp == 2, s8, s16)))
    wlane = jnp.where(grp == 0, 2.0, jnp.where(grp == 1, 4.0, jnp.where(grp == 2, 8.0, 16.0)))
    tpos = (t_idx * ts + lax.broadcasted_iota(jnp.int32, (ts, D_B), 0) + 1).astype(jnp.float32)
    cnt = jnp.minimum(tpos, wlane)
    pooled = ssel / cnt - up
    yb = jnp.dot(pooled.astype(MXU_DTYPE), wpool_ref[...], preferred_element_type=jnp.float32) * pscale_ref[...]

    out_ref[0, :, 0:D_B] = yb.astype(out_ref.dtype)
    out_ref[0, :, D_B:D_B + D_C] = yc.astype(out_ref.dtype)

    ext_c[0:HALO, :] = ext_c[ts:ts + HALO, :]
    ext_p[0:HALO, :] = ext_p[ts:ts + HALO, :]


def _bc_call(ubc, wpool_bd, pscale, convw, convb, lng, lnb, ts):
    b, s, _ = ubc.shape
    full2 = lambda bi, ti: (0, 0)
    return pl.pallas_call(
        _bc_kernel,
        out_shape=jax.ShapeDtypeStruct((b, s, D_B + D_C), MXU_DTYPE),
        grid=(b, s // ts),
        in_specs=[
            pl.BlockSpec((1, ts, C_END - C_UC), lambda bi, ti: (bi, ti, 0)),
            pl.BlockSpec((D_B, D_B), full2),
            pl.BlockSpec((1, D_B), full2),
            pl.BlockSpec((HALO, D_C), full2),
            pl.BlockSpec((1, D_C), full2),
            pl.BlockSpec((1, D_C), full2),
            pl.BlockSpec((1, D_C), full2),
        ],
        out_specs=pl.BlockSpec((1, ts, D_B + D_C), lambda bi, ti: (bi, ti, 0)),
        scratch_shapes=[pltpu.VMEM((ts + HALO, D_C), jnp.float32),
                        pltpu.VMEM((ts + HALO, D_B), jnp.float32)],
        compiler_params=_cparams(2),
        name="pool_conv",
    )(ubc, wpool_bd, pscale, convw, convb, lng, lnb)


KC = 1024
KA = 512
ONES_ROWS = 16
FOLD_ROWS = 64
SEL_BASE_PASSES = 13


def _dsa_kernel(ik_ref, ckv_ref, ckvT_ref, qlatT_ref, iqT_ref, wT_ref, wuvT_ref,
                ya_ref, sc_scr, acc_scr, m_scr, *, topk):
    i = pl.program_id(1)
    qpb = KC // Q_BLOCK
    nch = (i + qpb) // qpb
    q_pos = i * Q_BLOCK + lax.broadcasted_iota(jnp.int32, (1, LANES), 1)
    n_causal = q_pos + 1
    target = jnp.minimum(topk, n_causal)

    def fold(v, op):
        return op(v.reshape(KC // FOLD_ROWS, FOLD_ROWS, LANES), axis=0)

    def chunk_off(c):
        return pl.multiple_of(c * KC, KC)

    def key_pos(off):
        return off + lax.broadcasted_iota(jnp.int32, (KC, LANES), 0)

    def idx_chunk(c, stats, masked):
        off = chunk_off(c)
        lts = [jnp.dot(ik_ref[0, pl.ds(pl.multiple_of(off + k * KA, KA), KA), :], iqT_ref[0],
                       preferred_element_type=jnp.float32) for k in range(KC // KA)]
        parts = []
        for lt in lts:
            part = jnp.zeros((KA, LANES), jnp.float32)
            for h in range(N_IDX_HEADS):
                part = part + jnp.maximum(lt[:, h * LANES:(h + 1) * LANES], 0.0) * wT_ref[0, h:h + 1, :]
            parts.append(part)
        sc = jnp.concatenate(parts, axis=0)
        sc_for_min = sc
        if masked:
            causal = key_pos(off) <= q_pos
            sc_for_min = jnp.where(causal, sc, jnp.inf)
            sc = jnp.where(causal, sc, -jnp.inf)
        sc_scr[pl.ds(off, KC), :] = sc
        mx, mn, ge0, gt0 = stats
        return (jnp.maximum(mx, fold(sc, jnp.max)),
                jnp.minimum(mn, fold(sc_for_min, jnp.min)),
                ge0 + fold(jnp.where(sc >= 0.0, 1, 0).astype(jnp.int32), jnp.sum),
                gt0 + fold(jnp.where(sc > 0.0, 1, 0).astype(jnp.int32), jnp.sum))

    stats = (jnp.full((FOLD_ROWS, LANES), -jnp.inf, jnp.float32), jnp.full((FOLD_ROWS, LANES), jnp.inf, jnp.float32),
             jnp.zeros((FOLD_ROWS, LANES), jnp.int32), jnp.zeros((FOLD_ROWS, LANES), jnp.int32))
    stats = lax.fori_loop(0, nch - 1, lambda c, st: idx_chunk(c, st, False), stats)
    stats = idx_chunk(nch - 1, stats, True)
    rmax = jnp.max(stats[0], axis=0, keepdims=True)
    rmin = jnp.min(stats[1], axis=0, keepdims=True)
    c_ge0 = jnp.sum(stats[2], axis=0, keepdims=True)
    c_gt0 = jnp.sum(stats[3], axis=0, keepdims=True)

    def count_where(pred):
        def body(c, acc):
            off = chunk_off(c)
            ind = jnp.where(pred(sc_scr[pl.ds(off, KC), :], off), 1, 0).astype(jnp.int32)
            return acc + fold(ind, jnp.sum)
        acc = lax.fori_loop(0, nch, body, jnp.zeros((FOLD_ROWS, LANES), jnp.int32))
        return jnp.sum(acc, axis=0, keepdims=True)

    def unfinished(c_lo, fin):
        return (c_lo != target) & (fin == 0)

    def sel_pass(state):
        lo, hi, c_lo, fin = state
        mid = 0.5 * lo + 0.5 * hi
        mid = jnp.where(mid <= lo, hi, mid)
        active = unfinished(c_lo, fin)
        c = count_where(lambda blk, off: blk >= mid)
        ge = c >= target
        up = active & ge
        dn = active & jnp.logical_not(ge)
        fin = jnp.where(active & (mid == hi), 1, fin)
        return jnp.where(up, mid, lo), jnp.where(dn, mid, hi), jnp.where(up, c, c_lo), fin

    def any_lane(mask):
        return jnp.max(mask.astype(jnp.int32))

    at_zero = (c_gt0 < target) & (target <= c_ge0)
    above = c_gt0 >= target
    zero = jnp.zeros((1, LANES), jnp.float32)
    lo0 = jnp.where(at_zero | above, zero, rmin)
    hi0 = jnp.where(above, rmax, zero)
    c_lo0 = jnp.where(at_zero | above, c_ge0, n_causal)
    state0 = (lo0, hi0, c_lo0, at_zero.astype(jnp.int32))
    go0 = any_lane(unfinished(state0[2], state0[3]))

    n_blocks = i + 1
    n_fixed = SEL_BASE_PASSES + sum((n_blocks >= (1 << k)).astype(jnp.int32) for k in range(1, 16))
    state = lax.cond(go0 > 0,
                     lambda st: lax.fori_loop(0, n_fixed, lambda _, s_: sel_pass(s_), st),
                     lambda st: st, state0)

    def sel_more(carry):
        st = sel_pass(carry[0])
        return st, any_lane(unfinished(st[2], st[3]))

    state, _ = lax.while_loop(lambda carry: carry[1] > 0, sel_more,
                              (state, any_lane(unfinished(state[2], state[3]))))
    lo, _, c_lo, _ = state

    tie = c_lo > target

    @pl.when(any_lane(tie) > 0)
    def _():
        need = target - count_where(lambda blk, off: blk > lo)

        def j_body(_, st):
            j_lo, j_hi = st
            j_mid = (j_lo + j_hi) >> 1
            f = count_where(lambda blk, off: (blk == lo) & (key_pos(off) < j_mid))
            ok = f >= need
            return jnp.where(ok, j_lo, j_mid), jnp.where(ok, j_mid, j_hi)

        n_steps = int(sc_scr.shape[0]).bit_length()
        _, j_cut = lax.fori_loop(0, n_steps, j_body,
                                 (jnp.zeros((1, LANES), jnp.int32), jnp.full((1, LANES), nch * KC, jnp.int32)))

        def drop_body(c, carry):
            off = chunk_off(c)
            blk = sc_scr[pl.ds(off, KC), :]
            drop = tie & (blk == lo) & (key_pos(off) >= j_cut)
            sc_scr[pl.ds(off, KC), :] = jnp.where(drop, -jnp.inf, blk)
            return carry

        lax.fori_loop(0, nch, drop_body, 0)

    hq = N_HEADS_A * LANES
    eye_rep = jnp.where(lax.broadcasted_iota(jnp.int32, (LANES, hq), 0)
                        == lax.broadcasted_iota(jnp.int32, (LANES, hq), 1) % LANES, 1.0, 0.0).astype(MXU_DTYPE)
    q_aug = jnp.concatenate([qlatT_ref[0], eye_rep], axis=0)
    ones_rows = jnp.ones((ONES_ROWS, KA), MXU_DTYPE)
    m_scr[...] = jnp.full(m_scr.shape, NEG_BIG, jnp.float32)
    acc_scr[...] = jnp.zeros(acc_scr.shape, jnp.float32)

    def logits(off):
        bias = jnp.where(sc_scr[pl.ds(off, KA), :] >= lo, 0.0, NEG_BIG).astype(MXU_DTYPE)
        return jnp.dot(jnp.concatenate([ckv_ref[0, pl.ds(off, KA), :], bias], axis=1), q_aug,
                       preferred_element_type=jnp.float32)

    def softmax_pv(off, st):
        ps = []
        alphas = []
        for h in range(N_HEADS_A):
            sl = slice(h * LANES, (h + 1) * LANES)
            s = st[:, sl]
            m_old = m_scr[:, sl]
            m_new = jnp.maximum(m_old, jnp.max(s, axis=0, keepdims=True))
            m_scr[:, sl] = m_new
            ps.append(jnp.exp2(s - m_new).astype(MXU_DTYPE))
            alphas.append(jnp.exp2(m_old - m_new))
        p_all = jnp.concatenate(ps, axis=1)
        alpha_all = jnp.concatenate(alphas, axis=1)
        v_aug = jnp.concatenate([ckvT_ref[:, pl.ds(off, KA)], ones_rows], axis=0)
        acc_scr[...] = acc_scr[...] * alpha_all + jnp.dot(v_aug, p_all, preferred_element_type=jnp.float32)

    def att_body(c, carry):
        offs = [pl.multiple_of(c * KC + k * KA, KA) for k in range(KC // KA)]
        sts = [logits(off) for off in offs]
        for off, st in zip(offs, sts):
            softmax_pv(off, st)
        return carry

    lax.fori_loop(0, nch, att_body, 0)

    denom = acc_scr[KV_RANK:KV_RANK + 1, :]
    o_t = (acc_scr[0:KV_RANK, :] * (1.0 / denom)).astype(MXU_DTYPE)
    ys = []
    for h in range(N_HEADS_A):
        ys.append(jnp.dot(wuvT_ref[h], o_t[:, h * LANES:(h + 1) * LANES],
                          preferred_element_type=jnp.float32))
    ya_ref[0] = jnp.concatenate(ys, axis=0).T.astype(ya_ref.dtype)


def _dsa_call(ik, ckv, ckvT, qlatT, iqT, wT, wuvT, b, s):
    nb = s // Q_BLOCK
    hq = N_HEADS_A * LANES
    topk = min(TOPK_MAX, s // 4)
    return pl.pallas_call(
        functools.partial(_dsa_kernel, topk=topk),
        out_shape=jax.ShapeDtypeStruct((b, s, D_A), MXU_DTYPE),
        grid=(b, nb),
        in_specs=[
            pl.BlockSpec((1, s, LANES), lambda bi, qi: (bi, 0, 0)),
            pl.BlockSpec((1, s, KV_RANK), lambda bi, qi: (bi, 0, 0)),
            pl.BlockSpec((KV_RANK, s), lambda bi, qi: (0, bi)),
            pl.BlockSpec((1, KV_RANK, hq), lambda bi, qi: (bi * nb + qi, 0, 0)),
            pl.BlockSpec((1, LANES, hq), lambda bi, qi: (bi * nb + qi, 0, 0)),
            pl.BlockSpec((1, N_IDX_HEADS, LANES), lambda bi, qi: (bi * nb + qi, 0, 0)),
            pl.BlockSpec((N_HEADS_A, HEAD_DIM_A, KV_RANK), lambda bi, qi: (0, 0, 0)),
        ],
        out_specs=pl.BlockSpec((1, Q_BLOCK, D_A), lambda bi, qi: (bi, qi, 0)),
        scratch_shapes=[
            pltpu.VMEM((s, LANES), jnp.float32),
            pltpu.VMEM((KV_RANK + ONES_ROWS, hq), jnp.float32),
            pltpu.VMEM((1, hq), jnp.float32),
        ],
        compiler_params=_cparams(2),
        name="dsa_attention",
    )(ik, ckv, ckvT, qlatT, iqT, wT, wuvT)


def _out_kernel(x_ref, ya_ref, ybc_ref, wo_ref, g_ref, b_ref, rwT_ref, rbias_ref,
                x1_ref, gatesT_ref, sel_scr, aff_scr):
    mix = jnp.dot(ya_ref[...], wo_ref[0:D_A, :], preferred_element_type=jnp.float32)
    mix = mix + jnp.dot(ybc_ref[...], wo_ref[D_A:, :], preferred_element_type=jnp.float32)
    x1 = _layer_norm(ALPHA * x_ref[...] + mix, g_ref[...], b_ref[...])
    x1_ref[...] = x1

    logits_t = lax.dot_general(rwT_ref[...], x1, (((1,), (1,)), ((), ())),
                               precision=lax.Precision.HIGHEST,
                               preferred_element_type=jnp.float32)
    aff = _sigmoid(logits_t)
    aff_scr[...] = aff
    sel_scr[...] = aff + rbias_ref[...]

    def row(ref, e):
        return ref[e:e + 1, :]

    grp_scores = []
    for g in range(N_EXPERT_GROUPS):
        r = [row(sel_scr, g * EXPERTS_PER_GROUP + k) for k in range(EXPERTS_PER_GROUP)]
        m1, n1 = jnp.maximum(r[0], r[1]), jnp.minimum(r[0], r[1])
        m2, n2 = jnp.maximum(r[2], r[3]), jnp.minimum(r[2], r[3])
        grp_scores.append(jnp.maximum(m1, m2) + jnp.maximum(jnp.minimum(m1, m2), jnp.maximum(n1, n2)))
    best = jnp.zeros_like(grp_scores[0], dtype=jnp.int32)
    cur = grp_scores[0]
    for g in range(1, N_EXPERT_GROUPS):
        better = grp_scores[g] > cur
        best = jnp.where(better, g, best)
        cur = jnp.where(better, grp_scores[g], cur)

    sel_aff = []
    for e in range(N_EXPERTS):
        g = e // EXPERTS_PER_GROUP
        se = row(sel_scr, e)
        rank = jnp.zeros_like(best)
        for f in range(g * EXPERTS_PER_GROUP, (g + 1) * EXPERTS_PER_GROUP):
            if f == e:
                continue
            sf = row(sel_scr, f)
            ahead = (sf > se) | ((sf == se) & (f < e))
            rank = rank + ahead.astype(jnp.int32)
        chosen = (best == g) & (rank < 2)
        sel_aff.append(jnp.where(chosen, row(aff_scr, e), 0.0))
    denom = sel_aff[0]
    for e in range(1, N_EXPERTS):
        denom = denom + sel_aff[e]
    inv = 1.0 / denom
    for e in range(N_EXPERTS):
        gatesT_ref[e:e + 1, :] = sel_aff[e] * inv


def _out_call(xf, ya, ybc, wo, g, b, rwT, rbias, tm):
    n, d = xf.shape
    full = lambda i: (0, 0)
    return pl.pallas_call(
        _out_kernel,
        out_shape=(jax.ShapeDtypeStruct((n, d), jnp.float32),
                   jax.ShapeDtypeStruct((N_EXPERTS, n), jnp.float32)),
        grid=(n // tm,),
        in_specs=[
            pl.BlockSpec((tm, d), lambda i: (i, 0)),
            pl.BlockSpec((tm, D_A), lambda i: (i, 0)),
            pl.BlockSpec((tm, D_B + D_C), lambda i: (i, 0)),
            pl.BlockSpec((d, d), full),
            pl.BlockSpec((1, d), full),
            pl.BlockSpec((1, d), full),
            pl.BlockSpec((N_EXPERTS, d), full),
            pl.BlockSpec((N_EXPERTS, 1), full),
        ],
        out_specs=(pl.BlockSpec((tm, d), lambda i: (i, 0)),
                   pl.BlockSpec((N_EXPERTS, tm), lambda i: (0, i))),
        scratch_shapes=[pltpu.VMEM((N_EXPERTS, tm), jnp.float32),
                        pltpu.VMEM((N_EXPERTS, tm), jnp.float32)],
        compiler_params=_cparams(1),
        name="out_proj_router",
    )(xf, ya, ybc, wo, g, b, rwT, rbias)


MOE_CAP = 160


def _moe_kernel(x1_ref, gates_t_ref, tri_ref, wg_ref, wu_ref, wd_ref, g_ref, b_ref,
                out_ref, xb_scr, acc_scr, rank_scr, oh_scr, y_scr):
    e = pl.program_id(1)
    tm = x1_ref.shape[0]

    @pl.when(e == 0)
    def _():
        xb_scr[...] = x1_ref[...].astype(xb_scr.dtype)
        acc_scr[...] = jnp.zeros(acc_scr.shape, jnp.float32)
        routed_all = jnp.where(gates_t_ref[...] != 0.0, 1.0, 0.0).astype(MXU_DTYPE)
        rank_scr[...] = jnp.dot(routed_all, tri_ref[...], preferred_element_type=jnp.float32)

    g_row = gates_t_ref[pl.ds(e, 1), :]
    routed = g_row != 0.0
    slot = jnp.where(routed, rank_scr[pl.ds(e, 1), :].astype(jnp.int32), -1)
    n_routed = jnp.sum(routed.astype(jnp.int32))

    def expert_rows(b):
        row = b * MOE_CAP + lax.broadcasted_iota(jnp.int32, (MOE_CAP, tm), 0)
        hit = row == slot
        onehot = jnp.where(hit, 1.0, 0.0).astype(MXU_DTYPE)
        gate = jnp.sum(jnp.where(hit, g_row, 0.0), axis=1, keepdims=True)
        xe = jnp.dot(onehot, xb_scr[...], preferred_element_type=jnp.float32).astype(MXU_DTYPE)
        hg = jnp.dot(xe, wg_ref[0], preferred_element_type=jnp.float32)
        hu = jnp.dot(xe, wu_ref[0], preferred_element_type=jnp.float32)
        h = (hg * _sigmoid(hg)) * hu * gate
        y = jnp.dot(h.astype(MXU_DTYPE), wd_ref[0], preferred_element_type=jnp.float32)
        return onehot, y.astype(MXU_DTYPE)

    def scatter(onehot, y):
        return lax.dot_general(onehot, y, (((0,), (0,)), ((), ())), preferred_element_type=jnp.float32)

    base = pl.multiple_of(e * MOE_CAP, MOE_CAP)
    onehot0, y0 = expert_rows(0)
    oh_scr[pl.ds(base, MOE_CAP), :] = onehot0
    y_scr[pl.ds(base, MOE_CAP), :] = y0

    def overflow_body(b, carry):
        onehot, y = expert_rows(b)
        acc_scr[...] += scatter(onehot, y)
        return carry

    lax.fori_loop(1, (n_routed + MOE_CAP - 1) // MOE_CAP, overflow_body, 0)

    @pl.when(e == pl.num_programs(1) - 1)
    def _():
        ffn = acc_scr[...] + scatter(oh_scr[...], y_scr[...])
        out_ref[...] = _layer_norm(ALPHA * x1_ref[...] + ffn, g_ref[...], b_ref[...])


def _moe_call(x1, gates_t, wg, wu, wd, g, b, tm):
    n, d = x1.shape
    ne, _, de = wg.shape
    tri = jnp.triu(jnp.ones((tm, tm), jnp.float32), k=1).astype(MXU_DTYPE)
    return pl.pallas_call(
        _moe_kernel,
        out_shape=jax.ShapeDtypeStruct((n, d), jnp.float32),
        grid=(n // tm, ne),
        in_specs=[
            pl.BlockSpec((tm, d), lambda i, e: (i, 0)),
            pl.BlockSpec((ne, tm), lambda i, e: (0, i)),
            pl.BlockSpec((tm, tm), lambda i, e: (0, 0)),
            pl.BlockSpec((1, d, de), lambda i, e: (e, 0, 0)),
            pl.BlockSpec((1, d, de), lambda i, e: (e, 0, 0)),
            pl.BlockSpec((1, de, d), lambda i, e: (e, 0, 0)),
            pl.BlockSpec((1, d), lambda i, e: (0, 0)),
            pl.BlockSpec((1, d), lambda i, e: (0, 0)),
        ],
        out_specs=pl.BlockSpec((tm, d), lambda i, e: (i, 0)),
        scratch_shapes=[pltpu.VMEM((tm, d), MXU_DTYPE), pltpu.VMEM((tm, d), jnp.float32),
                        pltpu.VMEM((ne, tm), jnp.float32),
                        pltpu.VMEM((ne * MOE_CAP, tm), MXU_DTYPE), pltpu.VMEM((ne * MOE_CAP, d), MXU_DTYPE)],
        compiler_params=_cparams(2),
        name="moe_ffn",
    )(x1, gates_t, tri, wg, wu, wd, g, b)


def _arrange_w_in(w):
    d = w.shape[0]
    z = lambda k: jnp.zeros((d, k), w.dtype)
    idx_block = jnp.concatenate([w[:, _R_IK:_R_IW], z(IW_LANE - IDX_DIM), w[:, _R_IW:_R_POOL],
                                 z(LANES - IW_LANE - N_IDX_HEADS)], axis=1)
    return jnp.concatenate([w[:, _R_Q:_R_KV], w[:, _R_KV:_R_IQ], idx_block, w[:, _R_IQ:_R_IK],
                            w[:, _R_CONV:_R_END], w[:, _R_POOL:_R_CONV]], axis=1)


def _block_diag(w_pool):
    g, c, _ = w_pool.shape
    out = jnp.zeros((g * c, g * c), w_pool.dtype)
    for gi in range(g):
        out = out.at[gi * c:(gi + 1) * c, gi * c:(gi + 1) * c].set(w_pool[gi])
    return out


def _pick(n, pref):
    t = pref
    while n % t:
        t //= 2
    return t


def kernel(x, w_in, kv_norm_g, w_uk, w_uv, w_pool, pool_scale, conv_w, conv_b, conv_ln_g,
           conv_ln_b, w_o, ln1_g, ln1_b, router_w, router_bias, w_gate, w_up, w_down,
           ln2_g, ln2_b):
    b, s, d = x.shape
    n = b * s
    depth = w_in.shape[0]
    assert s % KC == 0 and d == 1024
    xf = x.reshape(n, d)
    rwT = router_w.T
    rbias = router_bias.reshape(N_EXPERTS, 1)
    tm_in = _pick(n, 512)
    ts_bc = _pick(s, 512)
    tm_out = _pick(n, 512)
    tm_moe = _pick(n, 1024)
    for l in range(depth):
        w_arr = _arrange_w_in(w_in[l]).astype(MXU_DTYPE)
        ubc, ckv, ckvT, ik, qlatT, iqT, wT = _in_call(
            xf, w_arr, kv_norm_g[l].reshape(1, KV_RANK), w_uk[l].astype(MXU_DTYPE), tm_in)
        convw = jnp.concatenate([conv_w[l].reshape(CONV_WIDTH, D_C),
                                 jnp.zeros((HALO - CONV_WIDTH, D_C), conv_w.dtype)], axis=0)
        ybc = _bc_call(ubc.reshape(b, s, C_END - C_UC), _block_diag(w_pool[l]).astype(MXU_DTYPE),
                       pool_scale[l].reshape(1, D_B), convw, conv_b[l].reshape(1, D_C),
                       conv_ln_g[l].reshape(1, D_C), conv_ln_b[l].reshape(1, D_C), ts_bc)
        wuvT = jnp.swapaxes(w_uv[l], 1, 2).astype(MXU_DTYPE)
        ya = _dsa_call(ik.reshape(b, s, LANES), ckv.reshape(b, s, KV_RANK), ckvT, qlatT, iqT, wT, wuvT, b, s)
        x1, gates_t = _out_call(xf, ya.reshape(n, D_A), ybc.reshape(n, D_B + D_C), w_o[l].astype(MXU_DTYPE),
                                ln1_g[l].reshape(1, d), ln1_b[l].reshape(1, d), rwT, rbias, tm_out)
        xf = _moe_call(x1, gates_t, w_gate[l].astype(MXU_DTYPE), w_up[l].astype(MXU_DTYPE),
                       w_down[l].astype(MXU_DTYPE), ln2_g[l].reshape(1, d), ln2_b[l].reshape(1, d), tm_moe)
    return xf.reshape(b, s, d)
```

```python
import functools
import math

import jax
import jax.numpy as jnp
from jax import lax
from jax.experimental import pallas as pl
from jax.experimental.pallas import tpu as pltpu

N_HEADS_A = 8
HEAD_DIM_A = 64
D_A = N_HEADS_A * HEAD_DIM_A
KV_RANK = 128
N_IDX_HEADS = 8
IDX_DIM = 32
TOPK_MAX = 256
Q_BLOCK = 128
N_POOL_GROUPS = 4
POOL_GROUP_DIM = 64
D_B = N_POOL_GROUPS * POOL_GROUP_DIM
POOL_WINDOWS = (2, 4, 8, 16)
D_C = 256
CONV_WIDTH = 31
N_EXPERTS = 16
N_EXPERT_GROUPS = 4
EXPERTS_PER_GROUP = N_EXPERTS // N_EXPERT_GROUPS
D_EXPERT = 512
DEPTH = 2
ALPHA = (2 * DEPTH) ** 0.25
LN_EPS = 1e-5

_R_Q = 0
_R_KV = _R_Q + D_A
_R_IQ = _R_KV + KV_RANK
_R_IK = _R_IQ + N_IDX_HEADS * IDX_DIM
_R_IW = _R_IK + IDX_DIM
_R_POOL = _R_IW + N_IDX_HEADS
_R_CONV = _R_POOL + D_B
_R_END = _R_CONV + 2 * D_C

LANES = 128
SUBLANES = 8
C_Q = 0
C_KV = C_Q + D_A
C_IDX = C_KV + KV_RANK
C_IQ = C_IDX + LANES
C_UC = C_IQ + N_IDX_HEADS * IDX_DIM
C_UP = C_UC + 2 * D_C
C_END = C_UP + D_B
IW_LANE = 96

MXU_DTYPE = jnp.bfloat16
VMEM_LIMIT = 56 * 1024 * 1024
NEG_BIG = -1e30
LOG2E = 1.4426950408889634


def _cparams(n_axes):
    return pltpu.CompilerParams(dimension_semantics=("arbitrary",) * n_axes,
                                vmem_limit_bytes=VMEM_LIMIT)


def _layer_norm(z, g, b):
    mu = jnp.mean(z, axis=-1, keepdims=True)
    zc = z - mu
    var = jnp.mean(zc * zc, axis=-1, keepdims=True)
    return zc * lax.rsqrt(var + LN_EPS) * g + b


def _sigmoid(v):
    return 1.0 / (1.0 + jnp.exp(-v))


def _in_kernel(x_ref, w_ref, g_ref, wuk_ref,
               ubc_ref, ckv_ref, ckvT_ref, ik_ref, qlatT_ref, iqT_ref, wT_ref, p_scr):
    tm = x_ref.shape[0]
    p_scr[...] = jnp.dot(x_ref[...].astype(MXU_DTYPE), w_ref[...], preferred_element_type=jnp.float32)

    ubc_ref[...] = p_scr[:, C_UC:C_END]

    c = p_scr[:, C_KV:C_KV + KV_RANK]
    ms = jnp.mean(c * c, axis=-1, keepdims=True)
    cn = c * lax.rsqrt(ms + LN_EPS) * g_ref[...]
    ckv_ref[...] = cn.astype(ckv_ref.dtype)
    ckvT_ref[...] = cn.T.astype(ckvT_ref.dtype)

    idx = p_scr[:, C_IDX:C_IDX + LANES]
    lane = lax.broadcasted_iota(jnp.int32, idx.shape, 1)
    ik_ref[...] = jnp.where(lane < IDX_DIM, idx, 0.0).astype(ik_ref.dtype)

    q_scale = HEAD_DIM_A ** -0.5 * LOG2E
    for j in range(tm // Q_BLOCK):
        r0 = j * Q_BLOCK
        iqb_t = (p_scr[r0:r0 + Q_BLOCK, C_IQ:C_IQ + N_IDX_HEADS * IDX_DIM] * (IDX_DIM ** -0.5)).T
        for h in range(N_IDX_HEADS):
            iqT_ref[j, 0:IDX_DIM, h * LANES:(h + 1) * LANES] = (
                iqb_t[h * IDX_DIM:(h + 1) * IDX_DIM, :].astype(iqT_ref.dtype))
        iqT_ref[j, IDX_DIM:LANES, :] = jnp.zeros((LANES - IDX_DIM, N_IDX_HEADS * LANES), iqT_ref.dtype)
        idx_t = p_scr[r0:r0 + Q_BLOCK, C_IDX:C_IDX + LANES].T
        wT_ref[j] = idx_t[IW_LANE:IW_LANE + N_IDX_HEADS, :] * (N_IDX_HEADS ** -0.5)
        for h in range(N_HEADS_A):
            qh = p_scr[r0:r0 + Q_BLOCK, C_Q + h * HEAD_DIM_A:C_Q + (h + 1) * HEAD_DIM_A].astype(MXU_DTYPE)
            t = lax.dot_general(wuk_ref[h], qh, (((1,), (1,)), ((), ())),
                                preferred_element_type=jnp.float32)
            qlatT_ref[j, :, h * LANES:(h + 1) * LANES] = (t * q_scale).astype(qlatT_ref.dtype)


def _in_call(xf, w_arr, kv_g, w_uk, tm):
    n, d = xf.shape
    nqb = n // Q_BLOCK
    jb = tm // Q_BLOCK
    hq = N_HEADS_A * LANES
    out_shape = (
        jax.ShapeDtypeStruct((n, C_END - C_UC), jnp.float32),
        jax.ShapeDtypeStruct((n, KV_RANK), MXU_DTYPE),
        jax.ShapeDtypeStruct((KV_RANK, n), MXU_DTYPE),
        jax.ShapeDtypeStruct((n, LANES), MXU_DTYPE),
        jax.ShapeDtypeStruct((nqb, KV_RANK, hq), MXU_DTYPE),
        jax.ShapeDtypeStruct((nqb, LANES, hq), MXU_DTYPE),
        jax.ShapeDtypeStruct((nqb, N_IDX_HEADS, LANES), jnp.float32),
    )
    return pl.pallas_call(
        _in_kernel,
        out_shape=out_shape,
        grid=(n // tm,),
        in_specs=[
            pl.BlockSpec((tm, d), lambda i: (i, 0)),
            pl.BlockSpec((d, C_END), lambda i: (0, 0)),
            pl.BlockSpec((1, KV_RANK), lambda i: (0, 0)),
            pl.BlockSpec((N_HEADS_A, KV_RANK, HEAD_DIM_A), lambda i: (0, 0, 0)),
        ],
        out_specs=(
            pl.BlockSpec((tm, C_END - C_UC), lambda i: (i, 0)),
            pl.BlockSpec((tm, KV_RANK), lambda i: (i, 0)),
            pl.BlockSpec((KV_RANK, tm), lambda i: (0, i)),
            pl.BlockSpec((tm, LANES), lambda i: (i, 0)),
            pl.BlockSpec((jb, KV_RANK, hq), lambda i: (i, 0, 0)),
            pl.BlockSpec((jb, LANES, hq), lambda i: (i, 0, 0)),
            pl.BlockSpec((jb, N_IDX_HEADS, LANES), lambda i: (i, 0, 0)),
        ),
        scratch_shapes=[pltpu.VMEM((tm, C_END), jnp.float32)],
        compiler_params=_cparams(1),
        name="in_proj",
    )(xf, w_arr, kv_g, w_uk)


HALO = 32


def _bc_kernel(ubc_ref, wpool_ref, pscale_ref, convw_ref, convb_ref, lng_ref, lnb_ref,
               out_ref, ext_c, ext_p, rot_c):
    t_idx = pl.program_id(1)
    ts = ubc_ref.shape[1]

    @pl.when(t_idx == 0)
    def _():
        ext_c[0:HALO, :] = jnp.zeros((HALO, D_C), jnp.float32)
        ext_p[0:HALO, :] = jnp.zeros((HALO, D_B), jnp.float32)

    a = ubc_ref[0, :, 0:D_C]
    gate = ubc_ref[0, :, D_C:2 * D_C]
    up = ubc_ref[0, :, 2 * D_C:2 * D_C + D_B]
    ext_c[HALO:HALO + ts, :] = a * _sigmoid(gate)
    ext_p[HALO:HALO + ts, :] = up

    base = HALO - (CONV_WIDTH - 1)
    span = ts + HALO - SUBLANES
    for r in range(1, SUBLANES):
        rot_c[r - 1, 0:span, :] = ext_c[r:r + span, :]
    conv = jnp.zeros((ts, D_C), jnp.float32) + convb_ref[...]
    for k in range(CONV_WIDTH):
        r, a = (base + k) % SUBLANES, (base + k) // SUBLANES * SUBLANES
        tap = ext_c[a:a + ts, :] if r == 0 else rot_c[r - 1, a:a + ts, :]
        conv = conv + tap * convw_ref[k:k + 1, :]
    hc = _layer_norm(conv, lng_ref[...], lnb_ref[...])
    yc = hc * _sigmoid(hc)

    def shifted(j):
        return ext_p[HALO - j:HALO - j + ts, :]
    s2 = shifted(0) + shifted(1)
    s4 = s2 + (shifted(2) + shifted(3))
    s8 = s4 + ((shifted(4) + shifted(5)) + (shifted(6) + shifted(7)))
    s16 = s8 + (((shifted(8) + shifted(9)) + (shifted(10) + shifted(11)))
                + ((shifted(12) + shifted(13)) + (shifted(14) + shifted(15))))
    grp = lax.broadcasted_iota(jnp.int32, (ts, D_B), 1) // POOL_GROUP_DIM
    ssel = jnp.where(grp == 0, s2, jnp.where(grp == 1, s4, jnp.where(grp == 2, s8, s16)))
    wlane = jnp.where(grp == 0, 2.0, jnp.where(grp == 1, 4.0, jnp.where(grp == 2, 8.0, 16.0)))
    tpos = (t_idx * ts + lax.broadcasted_iota(jnp.int32, (ts, D_B), 0) + 1).astype(jnp.float32)
    cnt = jnp.minimum(tpos, wlane)
    pooled = ssel / cnt - up
    yb = jnp.dot(pooled.astype(MXU_DTYPE), wpool_ref[...], preferred_element_type=jnp.float32) * pscale_ref[...]

    out_ref[0, :, 0:D_B] = yb.astype(out_ref.dtype)
    out_ref[0, :, D_B:D_B + D_C] = yc.astype(out_ref.dtype)

    ext_c[0:HALO, :] = ext_c[ts:ts + HALO, :]
    ext_p[0:HALO, :] = ext_p[ts:ts + HALO, :]


def _bc_call(ubc, wpool_bd, pscale, convw, convb, lng, lnb, ts):
    b, s, _ = ubc.shape
    full2 = lambda bi, ti: (0, 0)
    return pl.pallas_call(
        _bc_kernel,
        out_shape=jax.ShapeDtypeStruct((b, s, D_B + D_C), MXU_DTYPE),
        grid=(b, s // ts),
        in_specs=[
            pl.BlockSpec((1, ts, C_END - C_UC), lambda bi, ti: (bi, ti, 0)),
            pl.BlockSpec((D_B, D_B), full2),
            pl.BlockSpec((1, D_B), full2),
            pl.BlockSpec((HALO, D_C), full2),
            pl.BlockSpec((1, D_C), full2),
            pl.BlockSpec((1, D_C), full2),
            pl.BlockSpec((1, D_C), full2),
        ],
        out_specs=pl.BlockSpec((1, ts, D_B + D_C), lambda bi, ti: (bi, ti, 0)),
        scratch_shapes=[pltpu.VMEM((ts + HALO, D_C), jnp.float32),
                        pltpu.VMEM((ts + HALO, D_B), jnp.float32),
                        pltpu.VMEM((SUBLANES - 1, ts + HALO - SUBLANES, D_C), jnp.float32)],
        compiler_params=_cparams(2),
        name="pool_conv",
    )(ubc, wpool_bd, pscale, convw, convb, lng, lnb)


KA = 512
ONES_ROWS = 16
FOLD_ROWS = 64
SEL_BASE_PASSES = 13


def _dsa_kernel(ik_ref, ckv_ref, ckvT_ref, qlatT_ref, iqT_ref, wT_ref, wuvT_ref,
                ya_ref, sc_scr, acc_scr, m_scr, *, topk):
    i = pl.program_id(1)
    qps = KA // Q_BLOCK
    nsub = (i + qps) // qps
    q_pos = i * Q_BLOCK + lax.broadcasted_iota(jnp.int32, (1, LANES), 1)
    n_causal = q_pos + 1
    target = jnp.minimum(topk, n_causal)

    def fold(v, op):
        return op(v.reshape(KA // FOLD_ROWS, FOLD_ROWS, LANES), axis=0)

    def sub_off(j):
        return pl.multiple_of(j * KA, KA)

    def key_pos(off):
        return off + lax.broadcasted_iota(jnp.int32, (KA, LANES), 0)

    def idx_group(js, masks, stats):
        offs = [sub_off(j) for j in js]
        lts = [jnp.dot(ik_ref[0, pl.ds(off, KA), :], iqT_ref[0], preferred_element_type=jnp.float32)
               for off in offs]
        for off, lt, masked in zip(offs, lts, masks):
            sc = jnp.zeros((KA, LANES), jnp.float32)
            for h in range(N_IDX_HEADS):
                sc = sc + jnp.maximum(lt[:, h * LANES:(h + 1) * LANES], 0.0) * wT_ref[0, h:h + 1, :]
            sc_for_min = sc
            if masked:
                causal = key_pos(off) <= q_pos
                sc_for_min = jnp.where(causal, sc, jnp.inf)
                sc = jnp.where(causal, sc, -jnp.inf)
            sc_scr[pl.ds(off, KA), :] = sc
            mx, mn, ge0, gt0 = stats
            stats = (jnp.maximum(mx, fold(sc, jnp.max)),
                     jnp.minimum(mn, fold(sc_for_min, jnp.min)),
                     ge0 + fold(jnp.where(sc >= 0.0, 1, 0).astype(jnp.int32), jnp.sum),
                     gt0 + fold(jnp.where(sc > 0.0, 1, 0).astype(jnp.int32), jnp.sum))
        return stats

    stats = (jnp.full((FOLD_ROWS, LANES), -jnp.inf, jnp.float32), jnp.full((FOLD_ROWS, LANES), jnp.inf, jnp.float32),
             jnp.zeros((FOLD_ROWS, LANES), jnp.int32), jnp.zeros((FOLD_ROWS, LANES), jnp.int32))
    n_before = nsub - 1
    stats = lax.fori_loop(0, n_before // 2,
                          lambda p, st: idx_group([2 * p, 2 * p + 1], [False, False], st), stats)
    stats = lax.cond(n_before % 2 == 1,
                     lambda st: idx_group([nsub - 2, nsub - 1], [False, True], st),
                     lambda st: idx_group([nsub - 1], [True], st), stats)
    rmax = jnp.max(stats[0], axis=0, keepdims=True)
    rmin = jnp.min(stats[1], axis=0, keepdims=True)
    c_ge0 = jnp.sum(stats[2], axis=0, keepdims=True)
    c_gt0 = jnp.sum(stats[3], axis=0, keepdims=True)

    def count_where(pred):
        def body(j, acc):
            off = sub_off(j)
            ind = jnp.where(pred(sc_scr[pl.ds(off, KA), :], off), 1, 0).astype(jnp.int32)
            return acc + fold(ind, jnp.sum)
        acc = lax.fori_loop(0, nsub, body, jnp.zeros((FOLD_ROWS, LANES), jnp.int32))
        return jnp.sum(acc, axis=0, keepdims=True)

    def unfinished(c_lo, fin):
        return (c_lo != target) & (fin == 0)

    def sel_pass(state):
        lo, hi, c_lo, fin = state
        mid = 0.5 * lo + 0.5 * hi
        mid = jnp.where(mid <= lo, hi, mid)
        active = unfinished(c_lo, fin)
        c = count_where(lambda blk, off: blk >= mid)
        ge = c >= target
        up = active & ge
        dn = active & jnp.logical_not(ge)
        fin = jnp.where(active & (mid == hi), 1, fin)
        return jnp.where(up, mid, lo), jnp.where(dn, mid, hi), jnp.where(up, c, c_lo), fin

    def any_lane(mask):
        return jnp.max(mask.astype(jnp.int32))

    at_zero = (c_gt0 < target) & (target <= c_ge0)
    above = c_gt0 >= target
    zero = jnp.zeros((1, LANES), jnp.float32)
    lo0 = jnp.where(at_zero | above, zero, rmin)
    hi0 = jnp.where(above, rmax, zero)
    c_lo0 = jnp.where(at_zero | above, c_ge0, n_causal)
    state0 = (lo0, hi0, c_lo0, at_zero.astype(jnp.int32))
    go0 = any_lane(unfinished(state0[2], state0[3]))

    n_blocks = i + 1
    n_fixed = SEL_BASE_PASSES + sum((n_blocks >= (1 << k)).astype(jnp.int32) for k in range(1, 16))
    state = lax.cond(go0 > 0,
                     lambda st: lax.fori_loop(0, n_fixed, lambda _, s_: sel_pass(s_), st),
                     lambda st: st, state0)

    def sel_more(carry):
        st = sel_pass(carry[0])
        return st, any_lane(unfinished(st[2], st[3]))

    state, _ = lax.while_loop(lambda carry: carry[1] > 0, sel_more,
                              (state, any_lane(unfinished(state[2], state[3]))))
    lo, _, c_lo, _ = state

    tie = c_lo > target

    @pl.when(any_lane(tie) > 0)
    def _():
        need = target - count_where(lambda blk, off: blk > lo)

        def j_body(_, st):
            j_lo, j_hi = st
            j_mid = (j_lo + j_hi) >> 1
            f = count_where(lambda blk, off: (blk == lo) & (key_pos(off) < j_mid))
            ok = f >= need
            return jnp.where(ok, j_lo, j_mid), jnp.where(ok, j_mid, j_hi)

        n_steps = int(sc_scr.shape[0]).bit_length()
        _, j_cut = lax.fori_loop(0, n_steps, j_body,
                                 (jnp.zeros((1, LANES), jnp.int32), jnp.full((1, LANES), nsub * KA, jnp.int32)))

        def drop_body(j, carry):
            off = sub_off(j)
            blk = sc_scr[pl.ds(off, KA), :]
            drop = tie & (blk == lo) & (key_pos(off) >= j_cut)
            sc_scr[pl.ds(off, KA), :] = jnp.where(drop, -jnp.inf, blk)
            return carry

        lax.fori_loop(0, nsub, drop_body, 0)

    hq = N_HEADS_A * LANES
    eye_rep = jnp.where(lax.broadcasted_iota(jnp.int32, (LANES, hq), 0)
                        == lax.broadcasted_iota(jnp.int32, (LANES, hq), 1) % LANES, 1.0, 0.0).astype(MXU_DTYPE)
    q_aug = jnp.concatenate([qlatT_ref[0], eye_rep], axis=0)
    ones_rows = jnp.ones((ONES_ROWS, KA), MXU_DTYPE)
    m_scr[...] = jnp.full(m_scr.shape, NEG_BIG, jnp.float32)
    acc_scr[...] = jnp.zeros(acc_scr.shape, jnp.float32)

    def logits(off):
        bias = jnp.where(sc_scr[pl.ds(off, KA), :] >= lo, 0.0, NEG_BIG).astype(MXU_DTYPE)
        return jnp.dot(jnp.concatenate([ckv_ref[0, pl.ds(off, KA), :], bias], axis=1), q_aug,
                       preferred_element_type=jnp.float32)

    def softmax_pv(off, st):
        ps = []
        alphas = []
        for h in range(N_HEADS_A):
            sl = slice(h * LANES, (h + 1) * LANES)
            s = st[:, sl]
            m_old = m_scr[:, sl]
            m_new = jnp.maximum(m_old, jnp.max(s, axis=0, keepdims=True))
            m_scr[:, sl] = m_new
            ps.append(jnp.exp2(s - m_new).astype(MXU_DTYPE))
            alphas.append(jnp.exp2(m_old - m_new))
        p_all = jnp.concatenate(ps, axis=1)
        alpha_all = jnp.concatenate(alphas, axis=1)
        v_aug = jnp.concatenate([ckvT_ref[:, pl.ds(off, KA)], ones_rows], axis=0)
        acc_scr[...] = acc_scr[...] * alpha_all + jnp.dot(v_aug, p_all, preferred_element_type=jnp.float32)

    def att_group(js):
        offs = [sub_off(j) for j in js]
        sts = [logits(off) for off in offs]
        for off, st in zip(offs, sts):
            softmax_pv(off, st)

    def att_pair(p, carry):
        att_group([2 * p, 2 * p + 1])
        return carry

    lax.fori_loop(0, nsub // 2, att_pair, 0)

    @pl.when(nsub % 2 == 1)
    def _():
        att_group([nsub - 1])

    denom = acc_scr[KV_RANK:KV_RANK + 1, :]
    o_t = (acc_scr[0:KV_RANK, :] * (1.0 / denom)).astype(MXU_DTYPE)
    ys = []
    for h in range(N_HEADS_A):
        ys.append(jnp.dot(wuvT_ref[h], o_t[:, h * LANES:(h + 1) * LANES],
                          preferred_element_type=jnp.float32))
    ya_ref[0] = jnp.concatenate(ys, axis=0).T.astype(ya_ref.dtype)


def _dsa_call(ik, ckv, ckvT, qlatT, iqT, wT, wuvT, b, s):
    nb = s // Q_BLOCK
    hq = N_HEADS_A * LANES
    topk = min(TOPK_MAX, s // 4)
    return pl.pallas_call(
        functools.partial(_dsa_kernel, topk=topk),
        out_shape=jax.ShapeDtypeStruct((b, s, D_A), MXU_DTYPE),
        grid=(b, nb),
        in_specs=[
            pl.BlockSpec((1, s, LANES), lambda bi, qi: (bi, 0, 0)),
            pl.BlockSpec((1, s, KV_RANK), lambda bi, qi: (bi, 0, 0)),
            pl.BlockSpec((KV_RANK, s), lambda bi, qi: (0, bi)),
            pl.BlockSpec((1, KV_RANK, hq), lambda bi, qi: (bi * nb + qi, 0, 0)),
            pl.BlockSpec((1, LANES, hq), lambda bi, qi: (bi * nb + qi, 0, 0)),
            pl.BlockSpec((1, N_IDX_HEADS, LANES), lambda bi, qi: (bi * nb + qi, 0, 0)),
            pl.BlockSpec((N_HEADS_A, HEAD_DIM_A, KV_RANK), lambda bi, qi: (0, 0, 0)),
        ],
        out_specs=pl.BlockSpec((1, Q_BLOCK, D_A), lambda bi, qi: (bi, qi, 0)),
        scratch_shapes=[
            pltpu.VMEM((s, LANES), jnp.float32),
            pltpu.VMEM((KV_RANK + ONES_ROWS, hq), jnp.float32),
            pltpu.VMEM((1, hq), jnp.float32),
        ],
        compiler_params=_cparams(2),
        name="dsa_attention",
    )(ik, ckv, ckvT, qlatT, iqT, wT, wuvT)


def _out_kernel(x_ref, ya_ref, ybc_ref, wo_ref, g_ref, b_ref, rwT_ref, rbias_ref,
                x1_ref, gatesT_ref, sel_scr, aff_scr):
    mix = jnp.dot(ya_ref[...], wo_ref[0:D_A, :], preferred_element_type=jnp.float32)
    mix = mix + jnp.dot(ybc_ref[...], wo_ref[D_A:, :], preferred_element_type=jnp.float32)
    x1 = _layer_norm(ALPHA * x_ref[...] + mix, g_ref[...], b_ref[...])
    x1_ref[...] = x1

    def split(v):
        hi = v.astype(MXU_DTYPE)
        return hi, (v - hi.astype(jnp.float32)).astype(MXU_DTYPE)

    def dot_nt(a, b):
        return lax.dot_general(a, b, (((1,), (1,)), ((), ())), preferred_element_type=jnp.float32)

    r_hi, r_lo = split(rwT_ref[...])
    x_hi, x_lo = split(x1)
    logits_t = dot_nt(r_hi, x_hi) + (dot_nt(r_hi, x_lo) + dot_nt(r_lo, x_hi))
    aff = _sigmoid(logits_t)
    aff_scr[...] = aff
    sel_scr[...] = aff + rbias_ref[...]

    def row(ref, e):
        return ref[e:e + 1, :]

    grp_scores = []
    for g in range(N_EXPERT_GROUPS):
        r = [row(sel_scr, g * EXPERTS_PER_GROUP + k) for k in range(EXPERTS_PER_GROUP)]
        m1, n1 = jnp.maximum(r[0], r[1]), jnp.minimum(r[0], r[1])
        m2, n2 = jnp.maximum(r[2], r[3]), jnp.minimum(r[2], r[3])
        grp_scores.append(jnp.maximum(m1, m2) + jnp.maximum(jnp.minimum(m1, m2), jnp.maximum(n1, n2)))
    best = jnp.zeros_like(grp_scores[0], dtype=jnp.int32)
    cur = grp_scores[0]
    for g in range(1, N_EXPERT_GROUPS):
        better = grp_scores[g] > cur
        best = jnp.where(better, g, best)
        cur = jnp.where(better, grp_scores[g], cur)

    sel_aff = []
    for e in range(N_EXPERTS):
        g = e // EXPERTS_PER_GROUP
        se = row(sel_scr, e)
        rank = jnp.zeros_like(best)
        for f in range(g * EXPERTS_PER_GROUP, (g + 1) * EXPERTS_PER_GROUP):
            if f == e:
                continue
            sf = row(sel_scr, f)
            ahead = (sf > se) | ((sf == se) & (f < e))
            rank = rank + ahead.astype(jnp.int32)
        chosen = (best == g) & (rank < 2)
        sel_aff.append(jnp.where(chosen, row(aff_scr, e), 0.0))
    denom = sel_aff[0]
    for e in range(1, N_EXPERTS):
        denom = denom + sel_aff[e]
    inv = 1.0 / denom
    for e in range(N_EXPERTS):
        gatesT_ref[e:e + 1, :] = sel_aff[e] * inv


def _out_call(xf, ya, ybc, wo, g, b, rwT, rbias, tm):
    n, d = xf.shape
    full = lambda i: (0, 0)
    return pl.pallas_call(
        _out_kernel,
        out_shape=(jax.ShapeDtypeStruct((n, d), jnp.float32),
                   jax.ShapeDtypeStruct((N_EXPERTS, n), jnp.float32)),
        grid=(n // tm,),
        in_specs=[
            pl.BlockSpec((tm, d), lambda i: (i, 0)),
            pl.BlockSpec((tm, D_A), lambda i: (i, 0)),
            pl.BlockSpec((tm, D_B + D_C), lambda i: (i, 0)),
            pl.BlockSpec((d, d), full),
            pl.BlockSpec((1, d), full),
            pl.BlockSpec((1, d), full),
            pl.BlockSpec((N_EXPERTS, d), full),
            pl.BlockSpec((N_EXPERTS, 1), full),
        ],
        out_specs=(pl.BlockSpec((tm, d), lambda i: (i, 0)),
                   pl.BlockSpec((N_EXPERTS, tm), lambda i: (0, i))),
        scratch_shapes=[pltpu.VMEM((N_EXPERTS, tm), jnp.float32),
                        pltpu.VMEM((N_EXPERTS, tm), jnp.float32)],
        compiler_params=_cparams(1),
        name="out_proj_router",
    )(xf, ya, ybc, wo, g, b, rwT, rbias)


MOE_CAP = 160


def _moe_kernel(x1_ref, gates_t_ref, tri_ref, wg_ref, wu_ref, wd_ref, g_ref, b_ref,
                out_ref, xb_scr, acc_scr, rank_scr, oh_scr, y_scr):
    e = pl.program_id(1)
    tm = x1_ref.shape[0]

    @pl.when(e == 0)
    def _():
        xb_scr[...] = x1_ref[...].astype(xb_scr.dtype)
        acc_scr[...] = jnp.zeros(acc_scr.shape, jnp.float32)
        routed_all = jnp.where(gates_t_ref[...] != 0.0, 1.0, 0.0).astype(MXU_DTYPE)
        rank_scr[...] = jnp.dot(routed_all, tri_ref[...], preferred_element_type=jnp.float32)

    g_row = gates_t_ref[pl.ds(e, 1), :]
    routed = g_row != 0.0
    slot = jnp.where(routed, rank_scr[pl.ds(e, 1), :].astype(jnp.int32), -1)
    n_routed = jnp.sum(routed.astype(jnp.int32))

    def expert_rows(b):
        row = b * MOE_CAP + lax.broadcasted_iota(jnp.int32, (MOE_CAP, tm), 0)
        hit = row == slot
        onehot = jnp.where(hit, 1.0, 0.0).astype(MXU_DTYPE)
        gate = jnp.sum(jnp.where(hit, g_row, 0.0), axis=1, keepdims=True)
        xe = jnp.dot(onehot, xb_scr[...], preferred_element_type=jnp.float32).astype(MXU_DTYPE)
        hg = jnp.dot(xe, wg_ref[0, 0], preferred_element_type=jnp.float32)
        hu = jnp.dot(xe, wu_ref[0, 0], preferred_element_type=jnp.float32)
        h = (hg * _sigmoid(hg)) * hu * gate
        y = jnp.dot(h.astype(MXU_DTYPE), wd_ref[0, 0], preferred_element_type=jnp.float32)
        return onehot, y.astype(MXU_DTYPE)

    def scatter(onehot, y):
        return lax.dot_general(onehot, y, (((0,), (0,)), ((), ())), preferred_element_type=jnp.float32)

    base = pl.multiple_of(e * MOE_CAP, MOE_CAP)
    onehot0, y0 = expert_rows(0)
    oh_scr[pl.ds(base, MOE_CAP), :] = onehot0
    y_scr[pl.ds(base, MOE_CAP), :] = y0

    def overflow_body(b, carry):
        onehot, y = expert_rows(b)
        acc_scr[...] += scatter(onehot, y)
        return carry

    lax.fori_loop(1, (n_routed + MOE_CAP - 1) // MOE_CAP, overflow_body, 0)

    @pl.when(e == pl.num_programs(1) - 1)
    def _():
        ffn = acc_scr[...] + scatter(oh_scr[...], y_scr[...])
        out_ref[...] = _layer_norm(ALPHA * x1_ref[...] + ffn, g_ref[...], b_ref[...])


def _moe_call(x1, gates_t, wg, wu, wd, layer, g, b, tm):
    n, d = x1.shape
    _, ne, _, de = wg.shape
    tri = jnp.triu(jnp.ones((tm, tm), jnp.float32), k=1).astype(MXU_DTYPE)
    return pl.pallas_call(
        _moe_kernel,
        out_shape=jax.ShapeDtypeStruct((n, d), jnp.float32),
        grid=(n // tm, ne),
        in_specs=[
            pl.BlockSpec((tm, d), lambda i, e: (i, 0)),
            pl.BlockSpec((ne, tm), lambda i, e: (0, i)),
            pl.BlockSpec((tm, tm), lambda i, e: (0, 0)),
            pl.BlockSpec((1, 1, d, de), lambda i, e: (layer, e, 0, 0)),
            pl.BlockSpec((1, 1, d, de), lambda i, e: (layer, e, 0, 0)),
            pl.BlockSpec((1, 1, de, d), lambda i, e: (layer, e, 0, 0)),
            pl.BlockSpec((1, d), lambda i, e: (0, 0)),
            pl.BlockSpec((1, d), lambda i, e: (0, 0)),
        ],
        out_specs=pl.BlockSpec((tm, d), lambda i, e: (i, 0)),
        scratch_shapes=[pltpu.VMEM((tm, d), MXU_DTYPE), pltpu.VMEM((tm, d), jnp.float32),
                        pltpu.VMEM((ne, tm), jnp.float32),
                        pltpu.VMEM((ne * MOE_CAP, tm), MXU_DTYPE), pltpu.VMEM((ne * MOE_CAP, d), MXU_DTYPE)],
        compiler_params=_cparams(2),
        name="moe_ffn",
    )(x1, gates_t, tri, wg, wu, wd, g, b)


def _arrange_w_in(w):
    d = w.shape[0]
    z = lambda k: jnp.zeros((d, k), w.dtype)
    idx_block = jnp.concatenate([w[:, _R_IK:_R_IW], z(IW_LANE - IDX_DIM), w[:, _R_IW:_R_POOL],
                                 z(LANES - IW_LANE - N_IDX_HEADS)], axis=1)
    return jnp.concatenate([w[:, _R_Q:_R_KV], w[:, _R_KV:_R_IQ], idx_block, w[:, _R_IQ:_R_IK],
                            w[:, _R_CONV:_R_END], w[:, _R_POOL:_R_CONV]], axis=1)


def _block_diag(w_pool):
    g, c, _ = w_pool.shape
    out = jnp.zeros((g * c, g * c), w_pool.dtype)
    for gi in range(g):
        out = out.at[gi * c:(gi + 1) * c, gi * c:(gi + 1) * c].set(w_pool[gi])
    return out


def _pick(n, pref):
    t = pref
    while n % t:
        t //= 2
    return t


def kernel(x, w_in, kv_norm_g, w_uk, w_uv, w_pool, pool_scale, conv_w, conv_b, conv_ln_g,
           conv_ln_b, w_o, ln1_g, ln1_b, router_w, router_bias, w_gate, w_up, w_down,
           ln2_g, ln2_b):
    b, s, d = x.shape
    n = b * s
    depth = w_in.shape[0]
    assert s % KA == 0 and d == 1024
    xf = x.reshape(n, d)
    rwT = router_w.T
    rbias = router_bias.reshape(N_EXPERTS, 1)
    tm_in = _pick(n, 512)
    ts_bc = _pick(s, 512)
    tm_out = _pick(n, 512)
    tm_moe = _pick(n, 1024)
    wg_all, wu_all, wd_all = (w.astype(MXU_DTYPE) for w in (w_gate, w_up, w_down))
    for l in range(depth):
        w_arr = _arrange_w_in(w_in[l]).astype(MXU_DTYPE)
        ubc, ckv, ckvT, ik, qlatT, iqT, wT = _in_call(
            xf, w_arr, kv_norm_g[l].reshape(1, KV_RANK), w_uk[l].astype(MXU_DTYPE), tm_in)
        convw = jnp.concatenate([conv_w[l].reshape(CONV_WIDTH, D_C),
                                 jnp.zeros((HALO - CONV_WIDTH, D_C), conv_w.dtype)], axis=0)
        ybc = _bc_call(ubc.reshape(b, s, C_END - C_UC), _block_diag(w_pool[l]).astype(MXU_DTYPE),
                       pool_scale[l].reshape(1, D_B), convw, conv_b[l].reshape(1, D_C),
                       conv_ln_g[l].reshape(1, D_C), conv_ln_b[l].reshape(1, D_C), ts_bc)
        wuvT = jnp.swapaxes(w_uv[l], 1, 2).astype(MXU_DTYPE)
        ya = _dsa_call(ik.reshape(b, s, LANES), ckv.reshape(b, s, KV_RANK), ckvT, qlatT, iqT, wT, wuvT, b, s)
        x1, gates_t = _out_call(xf, ya.reshape(n, D_A), ybc.reshape(n, D_B + D_C), w_o[l].astype(MXU_DTYPE),
                                ln1_g[l].reshape(1, d), ln1_b[l].reshape(1, d), rwT, rbias, tm_out)
        xf = _moe_call(x1, gates_t, wg_all, wu_all, wd_all, l,
                       ln2_g[l].reshape(1, d), ln2_b[l].reshape(1, d), tm_moe)
    return xf.reshape(b, s, d)
```

```python
import functools
import math

import jax
import jax.numpy as jnp
from jax import lax
from jax.experimental import pallas as pl
from jax.experimental.pallas import tpu as pltpu

N_HEADS_A = 8
HEAD_DIM_A = 64
D_A = N_HEADS_A * HEAD_DIM_A
KV_RANK = 128
N_IDX_HEADS = 8
IDX_DIM = 32
TOPK_MAX = 256
Q_BLOCK = 128
N_POOL_GROUPS = 4
POOL_GROUP_DIM = 64
D_B = N_POOL_GROUPS * POOL_GROUP_DIM
POOL_WINDOWS = (2, 4, 8, 16)
D_C = 256
CONV_WIDTH = 31
N_EXPERTS = 16
N_EXPERT_GROUPS = 4
EXPERTS_PER_GROUP = N_EXPERTS // N_EXPERT_GROUPS
D_EXPERT = 512
DEPTH = 2
ALPHA = (2 * DEPTH) ** 0.25
LN_EPS = 1e-5

_R_Q = 0
_R_KV = _R_Q + D_A
_R_IQ = _R_KV + KV_RANK
_R_IK = _R_IQ + N_IDX_HEADS * IDX_DIM
_R_IW = _R_IK + IDX_DIM
_R_POOL = _R_IW + N_IDX_HEADS
_R_CONV = _R_POOL + D_B
_R_END = _R_CONV + 2 * D_C

LANES = 128
SUBLANES = 8
C_Q = 0
C_KV = C_Q + D_A
C_IDX = C_KV + KV_RANK
C_IQ = C_IDX + LANES
C_UC = C_IQ + N_IDX_HEADS * IDX_DIM
C_UP = C_UC + 2 * D_C
C_END = C_UP + D_B
IW_LANE = 96

MXU_DTYPE = jnp.bfloat16
VMEM_LIMIT = 56 * 1024 * 1024
NEG_BIG = -1e30
LOG2E = 1.4426950408889634


def _cparams(n_axes):
    return pltpu.CompilerParams(dimension_semantics=("arbitrary",) * n_axes,
                                vmem_limit_bytes=VMEM_LIMIT)


def _layer_norm(z, g, b):
    mu = jnp.mean(z, axis=-1, keepdims=True)
    zc = z - mu
    var = jnp.mean(zc * zc, axis=-1, keepdims=True)
    return zc * lax.rsqrt(var + LN_EPS) * g + b


def _sigmoid(v):
    return 1.0 / (1.0 + jnp.exp(-v))


def _in_kernel(x_ref, w_ref, g_ref, wuk_ref,
               ubc_ref, ckv_ref, ckvT_ref, ik_ref, qlatT_ref, iqT_ref, wT_ref, p_scr):
    tm = x_ref.shape[0]
    p_scr[...] = jnp.dot(x_ref[...].astype(MXU_DTYPE), w_ref[...], preferred_element_type=jnp.float32)

    ubc_ref[...] = p_scr[:, C_UC:C_END]

    c = p_scr[:, C_KV:C_KV + KV_RANK]
    ms = jnp.mean(c * c, axis=-1, keepdims=True)
    cn = c * lax.rsqrt(ms + LN_EPS) * g_ref[...]
    ckv_ref[...] = cn.astype(ckv_ref.dtype)
    ckvT_ref[...] = cn.T.astype(ckvT_ref.dtype)

    idx = p_scr[:, C_IDX:C_IDX + LANES]
    lane = lax.broadcasted_iota(jnp.int32, idx.shape, 1)
    ik_ref[...] = jnp.where(lane < IDX_DIM, idx, 0.0).astype(ik_ref.dtype)

    q_scale = HEAD_DIM_A ** -0.5 * LOG2E
    for j in range(tm // Q_BLOCK):
        r0 = j * Q_BLOCK
        iqb_t = (p_scr[r0:r0 + Q_BLOCK, C_IQ:C_IQ + N_IDX_HEADS * IDX_DIM] * (IDX_DIM ** -0.5)).T
        for h in range(N_IDX_HEADS):
            iqT_ref[j, 0:IDX_DIM, h * LANES:(h + 1) * LANES] = (
                iqb_t[h * IDX_DIM:(h + 1) * IDX_DIM, :].astype(iqT_ref.dtype))
        iqT_ref[j, IDX_DIM:LANES, :] = jnp.zeros((LANES - IDX_DIM, N_IDX_HEADS * LANES), iqT_ref.dtype)
        idx_t = p_scr[r0:r0 + Q_BLOCK, C_IDX:C_IDX + LANES].T
        wT_ref[j] = idx_t[IW_LANE:IW_LANE + N_IDX_HEADS, :] * (N_IDX_HEADS ** -0.5)
        for h in range(N_HEADS_A):
            qh = p_scr[r0:r0 + Q_BLOCK, C_Q + h * HEAD_DIM_A:C_Q + (h + 1) * HEAD_DIM_A].astype(MXU_DTYPE)
            t = lax.dot_general(wuk_ref[h], qh, (((1,), (1,)), ((), ())),
                                preferred_element_type=jnp.float32)
            qlatT_ref[j, :, h * LANES:(h + 1) * LANES] = (t * q_scale).astype(qlatT_ref.dtype)


def _in_call(xf, w_arr, kv_g, w_uk, tm):
    n, d = xf.shape
    nqb = n // Q_BLOCK
    jb = tm // Q_BLOCK
    hq = N_HEADS_A * LANES
    out_shape = (
        jax.ShapeDtypeStruct((n, C_END - C_UC), jnp.float32),
        jax.ShapeDtypeStruct((n, KV_RANK), MXU_DTYPE),
        jax.ShapeDtypeStruct((KV_RANK, n), MXU_DTYPE),
        jax.ShapeDtypeStruct((n, LANES), MXU_DTYPE),
        jax.ShapeDtypeStruct((nqb, KV_RANK, hq), MXU_DTYPE),
        jax.ShapeDtypeStruct((nqb, LANES, hq), MXU_DTYPE),
        jax.ShapeDtypeStruct((nqb, N_IDX_HEADS, LANES), jnp.float32),
    )
    return pl.pallas_call(
        _in_kernel,
        out_shape=out_shape,
        grid=(n // tm,),
        in_specs=[
            pl.BlockSpec((tm, d), lambda i: (i, 0)),
            pl.BlockSpec((d, C_END), lambda i: (0, 0)),
            pl.BlockSpec((1, KV_RANK), lambda i: (0, 0)),
            pl.BlockSpec((N_HEADS_A, KV_RANK, HEAD_DIM_A), lambda i: (0, 0, 0)),
        ],
        out_specs=(
            pl.BlockSpec((tm, C_END - C_UC), lambda i: (i, 0)),
            pl.BlockSpec((tm, KV_RANK), lambda i: (i, 0)),
            pl.BlockSpec((KV_RANK, tm), lambda i: (0, i)),
            pl.BlockSpec((tm, LANES), lambda i: (i, 0)),
            pl.BlockSpec((jb, KV_RANK, hq), lambda i: (i, 0, 0)),
            pl.BlockSpec((jb, LANES, hq), lambda i: (i, 0, 0)),
            pl.BlockSpec((jb, N_IDX_HEADS, LANES), lambda i: (i, 0, 0)),
        ),
        scratch_shapes=[pltpu.VMEM((tm, C_END), jnp.float32)],
        compiler_params=_cparams(1),
        name="in_proj",
    )(xf, w_arr, kv_g, w_uk)


HALO = 32


def _bc_kernel(ubc_ref, wpool_ref, pscale_ref, convw_ref, convb_ref, lng_ref, lnb_ref,
               out_ref, ext_c, ext_p, rot_c):
    t_idx = pl.program_id(1)
    ts = ubc_ref.shape[1]

    @pl.when(t_idx == 0)
    def _():
        ext_c[0:HALO, :] = jnp.zeros((HALO, D_C), jnp.float32)
        ext_p[0:HALO, :] = jnp.zeros((HALO, D_B), jnp.float32)

    a = ubc_ref[0, :, 0:D_C]
    gate = ubc_ref[0, :, D_C:2 * D_C]
    up = ubc_ref[0, :, 2 * D_C:2 * D_C + D_B]
    ext_c[HALO:HALO + ts, :] = a * _sigmoid(gate)
    ext_p[HALO:HALO + ts, :] = up

    base = HALO - (CONV_WIDTH - 1)
    span = ts + HALO - SUBLANES
    for r in range(1, SUBLANES):
        rot_c[r - 1, 0:span, :] = ext_c[r:r + span, :]
    conv = jnp.zeros((ts, D_C), jnp.float32) + convb_ref[...]
    for k in range(CONV_WIDTH):
        r, a = (base + k) % SUBLANES, (base + k) // SUBLANES * SUBLANES
        tap = ext_c[a:a + ts, :] if r == 0 else rot_c[r - 1, a:a + ts, :]
        conv = conv + tap * convw_ref[k:k + 1, :]
    hc = _layer_norm(conv, lng_ref[...], lnb_ref[...])
    yc = hc * _sigmoid(hc)

    def shifted(j):
        return ext_p[HALO - j:HALO - j + ts, :]
    s2 = shifted(0) + shifted(1)
    s4 = s2 + (shifted(2) + shifted(3))
    s8 = s4 + ((shifted(4) + shifted(5)) + (shifted(6) + shifted(7)))
    s16 = s8 + (((shifted(8) + shifted(9)) + (shifted(10) + shifted(11)))
                + ((shifted(12) + shifted(13)) + (shifted(14) + shifted(15))))
    grp = lax.broadcasted_iota(jnp.int32, (ts, D_B), 1) // POOL_GROUP_DIM
    ssel = jnp.where(grp == 0, s2, jnp.where(grp == 1, s4, jnp.where(grp == 2, s8, s16)))
    wlane = jnp.where(grp == 0, 2.0, jnp.where(grp == 1, 4.0, jnp.where(grp == 2, 8.0, 16.0)))
    tpos = (t_idx * ts + lax.broadcasted_iota(jnp.int32, (ts, D_B), 0) + 1).astype(jnp.float32)
    cnt = jnp.minimum(tpos, wlane)
    pooled = ssel / cnt - up
    yb = jnp.dot(pooled.astype(MXU_DTYPE), wpool_ref[...], preferred_element_type=jnp.float32) * pscale_ref[...]

    out_ref[0, :, 0:D_B] = yb.astype(out_ref.dtype)
    out_ref[0, :, D_B:D_B + D_C] = yc.astype(out_ref.dtype)

    ext_c[0:HALO, :] = ext_c[ts:ts + HALO, :]
    ext_p[0:HALO, :] = ext_p[ts:ts + HALO, :]


def _bc_call(ubc, wpool_bd, pscale, convw, convb, lng, lnb, ts):
    b, s, _ = ubc.shape
    full2 = lambda bi, ti: (0, 0)
    return pl.pallas_call(
        _bc_kernel,
        out_shape=jax.ShapeDtypeStruct((b, s, D_B + D_C), MXU_DTYPE),
        grid=(b, s // ts),
        in_specs=[
            pl.BlockSpec((1, ts, C_END - C_UC), lambda bi, ti: (bi, ti, 0)),
            pl.BlockSpec((D_B, D_B), full2),
            pl.BlockSpec((1, D_B), full2),
            pl.BlockSpec((HALO, D_C), full2),
            pl.BlockSpec((1, D_C), full2),
            pl.BlockSpec((1, D_C), full2),
            pl.BlockSpec((1, D_C), full2),
        ],
        out_specs=pl.BlockSpec((1, ts, D_B + D_C), lambda bi, ti: (bi, ti, 0)),
        scratch_shapes=[pltpu.VMEM((ts + HALO, D_C), jnp.float32),
                        pltpu.VMEM((ts + HALO, D_B), jnp.float32),
                        pltpu.VMEM((SUBLANES - 1, ts + HALO - SUBLANES, D_C), jnp.float32)],
        compiler_params=_cparams(2),
        name="pool_conv",
    )(ubc, wpool_bd, pscale, convw, convb, lng, lnb)


KA = 512
ONES_ROWS = 16
FOLD_ROWS = 64
SEL_BASE_PASSES = 13


def _dsa_kernel(ik_ref, ckv_ref, ckvT_ref, qlatT_ref, iqT_ref, wT_ref, wuvT_ref,
                ya_ref, sc_scr, acc_scr, m_scr, *, topk):
    i = pl.program_id(1)
    qps = KA // Q_BLOCK
    nsub = (i + qps) // qps
    q_pos = i * Q_BLOCK + lax.broadcasted_iota(jnp.int32, (1, LANES), 1)
    n_causal = q_pos + 1
    target = jnp.minimum(topk, n_causal)

    def fold(v, op):
        return op(v.reshape(KA // FOLD_ROWS, FOLD_ROWS, LANES), axis=0)

    def sub_off(j):
        return pl.multiple_of(j * KA, KA)

    def key_pos(off):
        return off + lax.broadcasted_iota(jnp.int32, (KA, LANES), 0)

    def idx_group(js, masks, stats):
        offs = [sub_off(j) for j in js]
        lts = [jnp.dot(ik_ref[0, pl.ds(off, KA), :], iqT_ref[0], preferred_element_type=jnp.float32)
               for off in offs]
        for off, lt, masked in zip(offs, lts, masks):
            sc = jnp.zeros((KA, LANES), jnp.float32)
            for h in range(N_IDX_HEADS):
                sc = sc + jnp.maximum(lt[:, h * LANES:(h + 1) * LANES], 0.0) * wT_ref[0, h:h + 1, :]
            sc_for_min = sc
            if masked:
                causal = key_pos(off) <= q_pos
                sc_for_min = jnp.where(causal, sc, jnp.inf)
                sc = jnp.where(causal, sc, -jnp.inf)
            sc_scr[pl.ds(off, KA), :] = sc
            mx, mn, ge0, gt0 = stats
            stats = (jnp.maximum(mx, fold(sc, jnp.max)),
                     jnp.minimum(mn, fold(sc_for_min, jnp.min)),
                     ge0 + fold(jnp.where(sc >= 0.0, 1, 0).astype(jnp.int32), jnp.sum),
                     gt0 + fold(jnp.where(sc > 0.0, 1, 0).astype(jnp.int32), jnp.sum))
        return stats

    stats = (jnp.full((FOLD_ROWS, LANES), -jnp.inf, jnp.float32), jnp.full((FOLD_ROWS, LANES), jnp.inf, jnp.float32),
             jnp.zeros((FOLD_ROWS, LANES), jnp.int32), jnp.zeros((FOLD_ROWS, LANES), jnp.int32))
    n_before = nsub - 1
    stats = lax.fori_loop(0, n_before // 2,
                          lambda p, st: idx_group([2 * p, 2 * p + 1], [False, False], st), stats)
    stats = lax.cond(n_before % 2 == 1,
                     lambda st: idx_group([nsub - 2, nsub - 1], [False, True], st),
                     lambda st: idx_group([nsub - 1], [True], st), stats)
    rmax = jnp.max(stats[0], axis=0, keepdims=True)
    rmin = jnp.min(stats[1], axis=0, keepdims=True)
    c_ge0 = jnp.sum(stats[2], axis=0, keepdims=True)
    c_gt0 = jnp.sum(stats[3], axis=0, keepdims=True)

    def count_where(pred):
        def body(j, acc):
            off = sub_off(j)
            ind = jnp.where(pred(sc_scr[pl.ds(off, KA), :], off), 1, 0).astype(jnp.int32)
            return acc + fold(ind, jnp.sum)
        acc = lax.fori_loop(0, nsub, body, jnp.zeros((FOLD_ROWS, LANES), jnp.int32))
        return jnp.sum(acc, axis=0, keepdims=True)

    def unfinished(c_lo, fin):
        return (c_lo != target) & (fin == 0)

    def sel_pass(state):
        lo, hi, c_lo, fin = state
        mid = 0.5 * lo + 0.5 * hi
        mid = jnp.where(mid <= lo, hi, mid)
        active = unfinished(c_lo, fin)
        c = count_where(lambda blk, off: blk >= mid)
        ge = c >= target
        up = active & ge
        dn = active & jnp.logical_not(ge)
        fin = jnp.where(active & (mid == hi), 1, fin)
        return jnp.where(up, mid, lo), jnp.where(dn, mid, hi), jnp.where(up, c, c_lo), fin

    def any_lane(mask):
        return jnp.max(mask.astype(jnp.int32))

    at_zero = (c_gt0 < target) & (target <= c_ge0)
    above = c_gt0 >= target
    zero = jnp.zeros((1, LANES), jnp.float32)
    lo0 = jnp.where(at_zero | above, zero, rmin)
    hi0 = jnp.where(above, rmax, zero)
    c_lo0 = jnp.where(at_zero | above, c_ge0, n_causal)
    state0 = (lo0, hi0, c_lo0, at_zero.astype(jnp.int32))
    go0 = any_lane(unfinished(state0[2], state0[3]))

    n_blocks = i + 1
    n_fixed = SEL_BASE_PASSES + sum((n_blocks >= (1 << k)).astype(jnp.int32) for k in range(1, 16))
    state = lax.cond(go0 > 0,
                     lambda st: lax.fori_loop(0, n_fixed, lambda _, s_: sel_pass(s_), st),
                     lambda st: st, state0)

    def sel_more(carry):
        st = sel_pass(carry[0])
        return st, any_lane(unfinished(st[2], st[3]))

    state, _ = lax.while_loop(lambda carry: carry[1] > 0, sel_more,
                              (state, any_lane(unfinished(state[2], state[3]))))
    lo, _, c_lo, _ = state

    tie = c_lo > target

    @pl.when(any_lane(tie) > 0)
    def _():
        need = target - count_where(lambda blk, off: blk > lo)

        def j_body(_, st):
            j_lo, j_hi = st
            j_mid = (j_lo + j_hi) >> 1
            f = count_where(lambda blk, off: (blk == lo) & (key_pos(off) < j_mid))
            ok = f >= need
            return jnp.where(ok, j_lo, j_mid), jnp.where(ok, j_mid, j_hi)

        n_steps = int(sc_scr.shape[0]).bit_length()
        _, j_cut = lax.fori_loop(0, n_steps, j_body,
                                 (jnp.zeros((1, LANES), jnp.int32), jnp.full((1, LANES), nsub * KA, jnp.int32)))

        def drop_body(j, carry):
            off = sub_off(j)
            blk = sc_scr[pl.ds(off, KA), :]
            drop = tie & (blk == lo) & (key_pos(off) >= j_cut)
            sc_scr[pl.ds(off, KA), :] = jnp.where(drop, -jnp.inf, blk)
            return carry

        lax.fori_loop(0, nsub, drop_body, 0)

    hq = N_HEADS_A * LANES
    eye_rep = jnp.where(lax.broadcasted_iota(jnp.int32, (LANES, hq), 0)
                        == lax.broadcasted_iota(jnp.int32, (LANES, hq), 1) % LANES, 1.0, 0.0).astype(MXU_DTYPE)
    q_aug = jnp.concatenate([qlatT_ref[0], eye_rep], axis=0)
    ones_rows = jnp.ones((ONES_ROWS, KA), MXU_DTYPE)
    m_scr[...] = jnp.full(m_scr.shape, NEG_BIG, jnp.float32)
    acc_scr[...] = jnp.zeros(acc_scr.shape, jnp.float32)

    def logits(off):
        bias = jnp.where(sc_scr[pl.ds(off, KA), :] >= lo, 0.0, NEG_BIG).astype(MXU_DTYPE)
        return jnp.dot(jnp.concatenate([ckv_ref[0, pl.ds(off, KA), :], bias], axis=1), q_aug,
                       preferred_element_type=jnp.float32)

    def softmax_pv(off, st):
        ps = []
        alphas = []
        for h in range(N_HEADS_A):
            sl = slice(h * LANES, (h + 1) * LANES)
            s = st[:, sl]
            m_old = m_scr[:, sl]
            m_new = jnp.maximum(m_old, jnp.max(s, axis=0, keepdims=True))
            m_scr[:, sl] = m_new
            ps.append(jnp.exp2(s - m_new).astype(MXU_DTYPE))
            alphas.append(jnp.exp2(m_old - m_new))
        p_all = jnp.concatenate(ps, axis=1)
        alpha_all = jnp.concatenate(alphas, axis=1)
        v_aug = jnp.concatenate([ckvT_ref[:, pl.ds(off, KA)], ones_rows], axis=0)
        acc_scr[...] = acc_scr[...] * alpha_all + jnp.dot(v_aug, p_all, preferred_element_type=jnp.float32)

    def att_group(js):
        offs = [sub_off(j) for j in js]
        sts = [logits(off) for off in offs]
        for off, st in zip(offs, sts):
            softmax_pv(off, st)

    def att_pair(p, carry):
        att_group([2 * p, 2 * p + 1])
        return carry

    lax.fori_loop(0, nsub // 2, att_pair, 0)

    @pl.when(nsub % 2 == 1)
    def _():
        att_group([nsub - 1])

    denom = acc_scr[KV_RANK:KV_RANK + 1, :]
    o_t = (acc_scr[0:KV_RANK, :] * (1.0 / denom)).astype(MXU_DTYPE)
    ys = []
    for h in range(N_HEADS_A):
        ys.append(jnp.dot(wuvT_ref[h], o_t[:, h * LANES:(h + 1) * LANES],
                          preferred_element_type=jnp.float32))
    ya_ref[0] = jnp.concatenate(ys, axis=0).T.astype(ya_ref.dtype)


def _dsa_call(ik, ckv, ckvT, qlatT, iqT, wT, wuvT, b, s):
    nb = s // Q_BLOCK
    hq = N_HEADS_A * LANES
    topk = min(TOPK_MAX, s // 4)
    return pl.pallas_call(
        functools.partial(_dsa_kernel, topk=topk),
        out_shape=jax.ShapeDtypeStruct((b, s, D_A), MXU_DTYPE),
        grid=(b, nb),
        in_specs=[
            pl.BlockSpec((1, s, LANES), lambda bi, qi: (bi, 0, 0)),
            pl.BlockSpec((1, s, KV_RANK), lambda bi, qi: (bi, 0, 0)),
            pl.BlockSpec((KV_RANK, s), lambda bi, qi: (0, bi)),
            pl.BlockSpec((1, KV_RANK, hq), lambda bi, qi: (bi * nb + qi, 0, 0)),
            pl.BlockSpec((1, LANES, hq), lambda bi, qi: (bi * nb + qi, 0, 0)),
            pl.BlockSpec((1, N_IDX_HEADS, LANES), lambda bi, qi: (bi * nb + qi, 0, 0)),
            pl.BlockSpec((N_HEADS_A, HEAD_DIM_A, KV_RANK), lambda bi, qi: (0, 0, 0)),
        ],
        out_specs=pl.BlockSpec((1, Q_BLOCK, D_A), lambda bi, qi: (bi, qi, 0)),
        scratch_shapes=[
            pltpu.VMEM((s, LANES), jnp.float32),
            pltpu.VMEM((KV_RANK + ONES_ROWS, hq), jnp.float32),
            pltpu.VMEM((1, hq), jnp.float32),
        ],
        compiler_params=_cparams(2),
        name="dsa_attention",
    )(ik, ckv, ckvT, qlatT, iqT, wT, wuvT)


def _out_kernel(x_ref, ya_ref, ybc_ref, wo_ref, g_ref, b_ref, rwT_ref, rbias_ref,
                x1_ref, gatesT_ref, sel_scr, aff_scr):
    mix = jnp.dot(ya_ref[...], wo_ref[0:D_A, :], preferred_element_type=jnp.float32)
    mix = mix + jnp.dot(ybc_ref[...], wo_ref[D_A:, :], preferred_element_type=jnp.float32)
    x1 = _layer_norm(ALPHA * x_ref[...] + mix, g_ref[...], b_ref[...])
    x1_ref[...] = x1

    def split(v):
        hi = v.astype(MXU_DTYPE)
        return hi, (v - hi.astype(jnp.float32)).astype(MXU_DTYPE)

    def dot_nt(a, b):
        return lax.dot_general(a, b, (((1,), (1,)), ((), ())), preferred_element_type=jnp.float32)

    r_hi, r_lo = split(rwT_ref[...])
    x_hi, x_lo = split(x1)
    logits_t = dot_nt(r_hi, x_hi) + (dot_nt(r_hi, x_lo) + dot_nt(r_lo, x_hi))
    aff = _sigmoid(logits_t)
    aff_scr[...] = aff
    sel_scr[...] = aff + rbias_ref[...]

    def row(ref, e):
        return ref[e:e + 1, :]

    grp_scores = []
    for g in range(N_EXPERT_GROUPS):
        r = [row(sel_scr, g * EXPERTS_PER_GROUP + k) for k in range(EXPERTS_PER_GROUP)]
        m1, n1 = jnp.maximum(r[0], r[1]), jnp.minimum(r[0], r[1])
        m2, n2 = jnp.maximum(r[2], r[3]), jnp.minimum(r[2], r[3])
        grp_scores.append(jnp.maximum(m1, m2) + jnp.maximum(jnp.minimum(m1, m2), jnp.maximum(n1, n2)))
    best = jnp.zeros_like(grp_scores[0], dtype=jnp.int32)
    cur = grp_scores[0]
    for g in range(1, N_EXPERT_GROUPS):
        better = grp_scores[g] > cur
        best = jnp.where(better, g, best)
        cur = jnp.where(better, grp_scores[g], cur)

    sel_aff = []
    for e in range(N_EXPERTS):
        g = e // EXPERTS_PER_GROUP
        se = row(sel_scr, e)
        rank = jnp.zeros_like(best)
        for f in range(g * EXPERTS_PER_GROUP, (g + 1) * EXPERTS_PER_GROUP):
            if f == e:
                continue
            sf = row(sel_scr, f)
            ahead = (sf > se) | ((sf == se) & (f < e))
            rank = rank + ahead.astype(jnp.int32)
        chosen = (best == g) & (rank < 2)
        sel_aff.append(jnp.where(chosen, row(aff_scr, e), 0.0))
    denom = sel_aff[0]
    for e in range(1, N_EXPERTS):
        denom = denom + sel_aff[e]
    inv = 1.0 / denom
    for e in range(N_EXPERTS):
        gatesT_ref[e:e + 1, :] = sel_aff[e] * inv


def _out_call(xf, ya, ybc, wo, g, b, rwT, rbias, tm):
    n, d = xf.shape
    full = lambda i: (0, 0)
    return pl.pallas_call(
        _out_kernel,
        out_shape=(jax.ShapeDtypeStruct((n, d), jnp.float32),
                   jax.ShapeDtypeStruct((N_EXPERTS, n), jnp.float32)),
        grid=(n // tm,),
        in_specs=[
            pl.BlockSpec((tm, d), lambda i: (i, 0)),
            pl.BlockSpec((tm, D_A), lambda i: (i, 0)),
            pl.BlockSpec((tm, D_B + D_C), lambda i: (i, 0)),
            pl.BlockSpec((d, d), full),
            pl.BlockSpec((1, d), full),
            pl.BlockSpec((1, d), full),
            pl.BlockSpec((N_EXPERTS, d), full),
            pl.BlockSpec((N_EXPERTS, 1), full),
        ],
        out_specs=(pl.BlockSpec((tm, d), lambda i: (i, 0)),
                   pl.BlockSpec((N_EXPERTS, tm), lambda i: (0, i))),
        scratch_shapes=[pltpu.VMEM((N_EXPERTS, tm), jnp.float32),
                        pltpu.VMEM((N_EXPERTS, tm), jnp.float32)],
        compiler_params=_cparams(1),
        name="out_proj_router",
    )(xf, ya, ybc, wo, g, b, rwT, rbias)


MOE_SUB = 256
MOE_CAP = 48


def _moe_kernel(x1_ref, gates_t_ref, tri_ref, wg_ref, wu_ref, wd_ref, g_ref, b_ref,
                out_ref, xb_scr, acc_scr, rank_scr, oh_scr, y_scr):
    e = pl.program_id(1)
    tm = x1_ref.shape[0]
    n_sub = tm // MOE_SUB

    @pl.when(e == 0)
    def _():
        xb_scr[...] = x1_ref[...].astype(xb_scr.dtype)
        acc_scr[...] = jnp.zeros(acc_scr.shape, jnp.float32)
        routed_all = jnp.where(gates_t_ref[...] != 0.0, 1.0, 0.0).astype(MXU_DTYPE)
        rank_scr[...] = jnp.dot(routed_all, tri_ref[...], preferred_element_type=jnp.float32)

    g_row = gates_t_ref[pl.ds(e, 1), :]
    routed = g_row != 0.0
    slot = jnp.where(routed, rank_scr[pl.ds(e, 1), :].astype(jnp.int32), -1)
    n_pass = (jnp.max(slot) + MOE_CAP) // MOE_CAP

    def scatter(onehot, y):
        return lax.dot_general(onehot, y, (((0,), (0,)), ((), ())), preferred_element_type=jnp.float32)

    def expert_rows(b):
        row = b * MOE_CAP + lax.broadcasted_iota(jnp.int32, (MOE_CAP, MOE_SUB), 0)
        onehots, gates, xes = [], [], []
        for j in range(n_sub):
            sl = slice(j * MOE_SUB, (j + 1) * MOE_SUB)
            hit = row == slot[:, sl]
            onehot = jnp.where(hit, 1.0, 0.0).astype(MXU_DTYPE)
            onehots.append(onehot)
            gates.append(jnp.sum(jnp.where(hit, g_row[:, sl], 0.0), axis=1, keepdims=True))
            xes.append(jnp.dot(onehot, xb_scr[sl, :], preferred_element_type=jnp.float32).astype(MXU_DTYPE))
        xe = jnp.concatenate(xes, axis=0)
        hg = jnp.dot(xe, wg_ref[0, 0], preferred_element_type=jnp.float32)
        hu = jnp.dot(xe, wu_ref[0, 0], preferred_element_type=jnp.float32)
        h = (hg * _sigmoid(hg)) * hu * jnp.concatenate(gates, axis=0)
        y = jnp.dot(h.astype(MXU_DTYPE), wd_ref[0, 0], preferred_element_type=jnp.float32)
        y = y.astype(MXU_DTYPE)
        return onehots, [y[j * MOE_CAP:(j + 1) * MOE_CAP] for j in range(n_sub)]

    base = pl.multiple_of(e * MOE_CAP, MOE_CAP)
    onehots0, ys0 = expert_rows(0)
    for j in range(n_sub):
        oh_scr[j, pl.ds(base, MOE_CAP), :] = onehots0[j]
        y_scr[j, pl.ds(base, MOE_CAP), :] = ys0[j]

    def overflow_body(b, carry):
        onehots, ys = expert_rows(b)
        for j in range(n_sub):
            sl = slice(j * MOE_SUB, (j + 1) * MOE_SUB)
            acc_scr[sl, :] += scatter(onehots[j], ys[j])
        return carry

    lax.fori_loop(1, n_pass, overflow_body, 0)

    @pl.when(e == pl.num_programs(1) - 1)
    def _():
        for j in range(n_sub):
            sl = slice(j * MOE_SUB, (j + 1) * MOE_SUB)
            ffn = acc_scr[sl, :] + scatter(oh_scr[j], y_scr[j])
            out_ref[sl, :] = _layer_norm(ALPHA * x1_ref[sl, :] + ffn, g_ref[...], b_ref[...])


def _moe_call(x1, gates_t, wg, wu, wd, layer, g, b, tm):
    n, d = x1.shape
    _, ne, _, de = wg.shape
    assert tm % MOE_SUB == 0
    t_idx = jnp.arange(tm)
    tri = ((t_idx[:, None] < t_idx[None, :]) & (t_idx[:, None] // MOE_SUB == t_idx[None, :] // MOE_SUB)
           ).astype(MXU_DTYPE)
    n_sub = tm // MOE_SUB
    return pl.pallas_call(
        _moe_kernel,
        out_shape=jax.ShapeDtypeStruct((n, d), jnp.float32),
        grid=(n // tm, ne),
        in_specs=[
            pl.BlockSpec((tm, d), lambda i, e: (i, 0)),
            pl.BlockSpec((ne, tm), lambda i, e: (0, i)),
            pl.BlockSpec((tm, tm), lambda i, e: (0, 0)),
            pl.BlockSpec((1, 1, d, de), lambda i, e: (layer, e, 0, 0)),
            pl.BlockSpec((1, 1, d, de), lambda i, e: (layer, e, 0, 0)),
            pl.BlockSpec((1, 1, de, d), lambda i, e: (layer, e, 0, 0)),
            pl.BlockSpec((1, d), lambda i, e: (0, 0)),
            pl.BlockSpec((1, d), lambda i, e: (0, 0)),
        ],
        out_specs=pl.BlockSpec((tm, d), lambda i, e: (i, 0)),
        scratch_shapes=[pltpu.VMEM((tm, d), MXU_DTYPE), pltpu.VMEM((tm, d), jnp.float32),
                        pltpu.VMEM((ne, tm), jnp.float32),
                        pltpu.VMEM((n_sub, ne * MOE_CAP, MOE_SUB), MXU_DTYPE),
                        pltpu.VMEM((n_sub, ne * MOE_CAP, d), MXU_DTYPE)],
        compiler_params=_cparams(2),
        name="moe_ffn",
    )(x1, gates_t, tri, wg, wu, wd, g, b)


def _arrange_w_in(w):
    d = w.shape[0]
    z = lambda k: jnp.zeros((d, k), w.dtype)
    idx_block = jnp.concatenate([w[:, _R_IK:_R_IW], z(IW_LANE - IDX_DIM), w[:, _R_IW:_R_POOL],
                                 z(LANES - IW_LANE - N_IDX_HEADS)], axis=1)
    return jnp.concatenate([w[:, _R_Q:_R_KV], w[:, _R_KV:_R_IQ], idx_block, w[:, _R_IQ:_R_IK],
                            w[:, _R_CONV:_R_END], w[:, _R_POOL:_R_CONV]], axis=1)


def _block_diag(w_pool):
    g, c, _ = w_pool.shape
    out = jnp.zeros((g * c, g * c), w_pool.dtype)
    for gi in range(g):
        out = out.at[gi * c:(gi + 1) * c, gi * c:(gi + 1) * c].set(w_pool[gi])
    return out


def _pick(n, pref):
    t = pref
    while n % t:
        t //= 2
    return t


def kernel(x, w_in, kv_norm_g, w_uk, w_uv, w_pool, pool_scale, conv_w, conv_b, conv_ln_g,
           conv_ln_b, w_o, ln1_g, ln1_b, router_w, router_bias, w_gate, w_up, w_down,
           ln2_g, ln2_b):
    b, s, d = x.shape
    n = b * s
    depth = w_in.shape[0]
    assert s % KA == 0 and d == 1024
    xf = x.reshape(n, d)
    rwT = router_w.T
    rbias = router_bias.reshape(N_EXPERTS, 1)
    tm_in = _pick(n, 512)
    ts_bc = _pick(s, 512)
    tm_out = _pick(n, 512)
    tm_moe = _pick(n, 1024)
    wg_all, wu_all, wd_all = (w.astype(MXU_DTYPE) for w in (w_gate, w_up, w_down))
    for l in range(depth):
        w_arr = _arrange_w_in(w_in[l]).astype(MXU_DTYPE)
        ubc, ckv, ckvT, ik, qlatT, iqT, wT = _in_call(
            xf, w_arr, kv_norm_g[l].reshape(1, KV_RANK), w_uk[l].astype(MXU_DTYPE), tm_in)
        convw = jnp.concatenate([conv_w[l].reshape(CONV_WIDTH, D_C),
                                 jnp.zeros((HALO - CONV_WIDTH, D_C), conv_w.dtype)], axis=0)
        ybc = _bc_call(ubc.reshape(b, s, C_END - C_UC), _block_diag(w_pool[l]).astype(MXU_DTYPE),
                       pool_scale[l].reshape(1, D_B), convw, conv_b[l].reshape(1, D_C),
                       conv_ln_g[l].reshape(1, D_C), conv_ln_b[l].reshape(1, D_C), ts_bc)
        wuvT = jnp.swapaxes(w_uv[l], 1, 2).astype(MXU_DTYPE)
        ya = _dsa_call(ik.reshape(b, s, LANES), ckv.reshape(b, s, KV_RANK), ckvT, qlatT, iqT, wT, wuvT, b, s)
        x1, gates_t = _out_call(xf, ya.reshape(n, D_A), ybc.reshape(n, D_B + D_C), w_o[l].astype(MXU_DTYPE),
                                ln1_g[l].reshape(1, d), ln1_b[l].reshape(1, d), rwT, rbias, tm_out)
        xf = _moe_call(x1, gates_t, wg_all, wu_all, wd_all, l,
                       ln2_g[l].reshape(1, d), ln2_b[l].reshape(1, d), tm_moe)
    return xf.reshape(b, s, d)
```

```python
import functools
import math

import jax
import jax.numpy as jnp
from jax import lax
from jax.experimental import pallas as pl
from jax.experimental.pallas import tpu as pltpu

N_HEADS_A = 8
HEAD_DIM_A = 64
D_A = N_HEADS_A * HEAD_DIM_A
KV_RANK = 128
N_IDX_HEADS = 8
IDX_DIM = 32
TOPK_MAX = 256
Q_BLOCK = 128
N_POOL_GROUPS = 4
POOL_GROUP_DIM = 64
D_B = N_POOL_GROUPS * POOL_GROUP_DIM
POOL_WINDOWS = (2, 4, 8, 16)
D_C = 256
CONV_WIDTH = 31
N_EXPERTS = 16
N_EXPERT_GROUPS = 4
EXPERTS_PER_GROUP = N_EXPERTS // N_EXPERT_GROUPS
D_EXPERT = 512
DEPTH = 2
ALPHA = (2 * DEPTH) ** 0.25
LN_EPS = 1e-5

_R_Q = 0
_R_KV = _R_Q + D_A
_R_IQ = _R_KV + KV_RANK
_R_IK = _R_IQ + N_IDX_HEADS * IDX_DIM
_R_IW = _R_IK + IDX_DIM
_R_POOL = _R_IW + N_IDX_HEADS
_R_CONV = _R_POOL + D_B
_R_END = _R_CONV + 2 * D_C

LANES = 128
SUBLANES = 8
C_Q = 0
C_KV = C_Q + D_A
C_IDX = C_KV + KV_RANK
C_IQ = C_IDX + LANES
C_UC = C_IQ + N_IDX_HEADS * IDX_DIM
C_UP = C_UC + 2 * D_C
C_END = C_UP + D_B
IW_LANE = 96

MXU_DTYPE = jnp.bfloat16
VMEM_LIMIT = 56 * 1024 * 1024
NEG_BIG = -1e30
LOG2E = 1.4426950408889634


def _cparams(n_axes):
    return pltpu.CompilerParams(dimension_semantics=("arbitrary",) * n_axes,
                                vmem_limit_bytes=VMEM_LIMIT)


def _layer_norm(z, g, b):
    mu = jnp.mean(z, axis=-1, keepdims=True)
    zc = z - mu
    var = jnp.mean(zc * zc, axis=-1, keepdims=True)
    return zc * lax.rsqrt(var + LN_EPS) * g + b


def _sigmoid(v):
    return 1.0 / (1.0 + jnp.exp(-v))


def _in_kernel(x_ref, w_ref, g_ref, wuk_ref,
               ubc_ref, ckv_ref, ckvT_ref, ik_ref, qlatT_ref, iqT_ref, wT_ref, p_scr):
    tm = x_ref.shape[0]
    p_scr[...] = jnp.dot(x_ref[...].astype(MXU_DTYPE), w_ref[...], preferred_element_type=jnp.float32)

    ubc_ref[...] = p_scr[:, C_UC:C_END]

    c = p_scr[:, C_KV:C_KV + KV_RANK]
    ms = jnp.mean(c * c, axis=-1, keepdims=True)
    cn = c * lax.rsqrt(ms + LN_EPS) * g_ref[...]
    ckv_ref[...] = cn.astype(ckv_ref.dtype)
    ckvT_ref[...] = cn.T.astype(ckvT_ref.dtype)

    idx = p_scr[:, C_IDX:C_IDX + LANES]
    lane = lax.broadcasted_iota(jnp.int32, idx.shape, 1)
    ik_ref[...] = jnp.where(lane < IDX_DIM, idx, 0.0).astype(ik_ref.dtype)

    q_scale = HEAD_DIM_A ** -0.5 * LOG2E
    for j in range(tm // Q_BLOCK):
        r0 = j * Q_BLOCK
        iqb_t = (p_scr[r0:r0 + Q_BLOCK, C_IQ:C_IQ + N_IDX_HEADS * IDX_DIM] * (IDX_DIM ** -0.5)).T
        for h in range(N_IDX_HEADS):
            iqT_ref[j, 0:IDX_DIM, h * LANES:(h + 1) * LANES] = (
                iqb_t[h * IDX_DIM:(h + 1) * IDX_DIM, :].astype(iqT_ref.dtype))
        iqT_ref[j, IDX_DIM:LANES, :] = jnp.zeros((LANES - IDX_DIM, N_IDX_HEADS * LANES), iqT_ref.dtype)
        idx_t = p_scr[r0:r0 + Q_BLOCK, C_IDX:C_IDX + LANES].T
        wT_ref[j] = idx_t[IW_LANE:IW_LANE + N_IDX_HEADS, :] * (N_IDX_HEADS ** -0.5)
        for h in range(N_HEADS_A):
            qh = p_scr[r0:r0 + Q_BLOCK, C_Q + h * HEAD_DIM_A:C_Q + (h + 1) * HEAD_DIM_A].astype(MXU_DTYPE)
            t = lax.dot_general(wuk_ref[h], qh, (((1,), (1,)), ((), ())),
                                preferred_element_type=jnp.float32)
            qlatT_ref[j, :, h * LANES:(h + 1) * LANES] = (t * q_scale).astype(qlatT_ref.dtype)


def _in_call(xf, w_arr, kv_g, w_uk, tm):
    n, d = xf.shape
    nqb = n // Q_BLOCK
    jb = tm // Q_BLOCK
    hq = N_HEADS_A * LANES
    out_shape = (
        jax.ShapeDtypeStruct((n, C_END - C_UC), jnp.float32),
        jax.ShapeDtypeStruct((n, KV_RANK), MXU_DTYPE),
        jax.ShapeDtypeStruct((KV_RANK, n), MXU_DTYPE),
        jax.ShapeDtypeStruct((n, LANES), MXU_DTYPE),
        jax.ShapeDtypeStruct((nqb, KV_RANK, hq), MXU_DTYPE),
        jax.ShapeDtypeStruct((nqb, LANES, hq), MXU_DTYPE),
        jax.ShapeDtypeStruct((nqb, N_IDX_HEADS, LANES), jnp.float32),
    )
    return pl.pallas_call(
        _in_kernel,
        out_shape=out_shape,
        grid=(n // tm,),
        in_specs=[
            pl.BlockSpec((tm, d), lambda i: (i, 0)),
            pl.BlockSpec((d, C_END), lambda i: (0, 0)),
            pl.BlockSpec((1, KV_RANK), lambda i: (0, 0)),
            pl.BlockSpec((N_HEADS_A, KV_RANK, HEAD_DIM_A), lambda i: (0, 0, 0)),
        ],
        out_specs=(
            pl.BlockSpec((tm, C_END - C_UC), lambda i: (i, 0)),
            pl.BlockSpec((tm, KV_RANK), lambda i: (i, 0)),
            pl.BlockSpec((KV_RANK, tm), lambda i: (0, i)),
            pl.BlockSpec((tm, LANES), lambda i: (i, 0)),
            pl.BlockSpec((jb, KV_RANK, hq), lambda i: (i, 0, 0)),
            pl.BlockSpec((jb, LANES, hq), lambda i: (i, 0, 0)),
            pl.BlockSpec((jb, N_IDX_HEADS, LANES), lambda i: (i, 0, 0)),
        ),
        scratch_shapes=[pltpu.VMEM((tm, C_END), jnp.float32)],
        compiler_params=_cparams(1),
        name="in_proj",
    )(xf, w_arr, kv_g, w_uk)


HALO = 32


def _bc_kernel(ubc_ref, wpool_ref, pscale_ref, convw_ref, convb_ref, lng_ref, lnb_ref,
               out_ref, ext_c, ext_p, rot_c):
    t_idx = pl.program_id(1)
    ts = ubc_ref.shape[1]

    @pl.when(t_idx == 0)
    def _():
        ext_c[0:HALO, :] = jnp.zeros((HALO, D_C), jnp.float32)
        ext_p[0:HALO, :] = jnp.zeros((HALO, D_B), jnp.float32)

    a = ubc_ref[0, :, 0:D_C]
    gate = ubc_ref[0, :, D_C:2 * D_C]
    up = ubc_ref[0, :, 2 * D_C:2 * D_C + D_B]
    ext_c[HALO:HALO + ts, :] = a * _sigmoid(gate)
    ext_p[HALO:HALO + ts, :] = up

    base = HALO - (CONV_WIDTH - 1)
    span = ts + HALO - SUBLANES
    for r in range(1, SUBLANES):
        rot_c[r - 1, 0:span, :] = ext_c[r:r + span, :]
    conv = jnp.zeros((ts, D_C), jnp.float32) + convb_ref[...]
    for k in range(CONV_WIDTH):
        r, a = (base + k) % SUBLANES, (base + k) // SUBLANES * SUBLANES
        tap = ext_c[a:a + ts, :] if r == 0 else rot_c[r - 1, a:a + ts, :]
        conv = conv + tap * convw_ref[k:k + 1, :]
    hc = _layer_norm(conv, lng_ref[...], lnb_ref[...])
    yc = hc * _sigmoid(hc)

    def shifted(j):
        return ext_p[HALO - j:HALO - j + ts, :]
    s2 = shifted(0) + shifted(1)
    s4 = s2 + (shifted(2) + shifted(3))
    s8 = s4 + ((shifted(4) + shifted(5)) + (shifted(6) + shifted(7)))
    s16 = s8 + (((shifted(8) + shifted(9)) + (shifted(10) + shifted(11)))
                + ((shifted(12) + shifted(13)) + (shifted(14) + shifted(15))))
    grp = lax.broadcasted_iota(jnp.int32, (ts, D_B), 1) // POOL_GROUP_DIM
    ssel = jnp.where(grp == 0, s2, jnp.where(grp == 1, s4, jnp.where(grp == 2, s8, s16)))
    wlane = jnp.where(grp == 0, 2.0, jnp.where(grp == 1, 4.0, jnp.where(grp == 2, 8.0, 16.0)))
    tpos = (t_idx * ts + lax.broadcasted_iota(jnp.int32, (ts, D_B), 0) + 1).astype(jnp.float32)
    cnt = jnp.minimum(tpos, wlane)
    pooled = ssel / cnt - up
    yb = jnp.dot(pooled.astype(MXU_DTYPE), wpool_ref[...], preferred_element_type=jnp.float32) * pscale_ref[...]

    out_ref[0, :, 0:D_B] = yb.astype(out_ref.dtype)
    out_ref[0, :, D_B:D_B + D_C] = yc.astype(out_ref.dtype)

    ext_c[0:HALO, :] = ext_c[ts:ts + HALO, :]
    ext_p[0:HALO, :] = ext_p[ts:ts + HALO, :]


def _bc_call(ubc, wpool_bd, pscale, convw, convb, lng, lnb, ts):
    b, s, _ = ubc.shape
    full2 = lambda bi, ti: (0, 0)
    return pl.pallas_call(
        _bc_kernel,
        out_shape=jax.ShapeDtypeStruct((b, s, D_B + D_C), MXU_DTYPE),
        grid=(b, s // ts),
        in_specs=[
            pl.BlockSpec((1, ts, C_END - C_UC), lambda bi, ti: (bi, ti, 0)),
            pl.BlockSpec((D_B, D_B), full2),
            pl.BlockSpec((1, D_B), full2),
            pl.BlockSpec((HALO, D_C), full2),
            pl.BlockSpec((1, D_C), full2),
            pl.BlockSpec((1, D_C), full2),
            pl.BlockSpec((1, D_C), full2),
        ],
        out_specs=pl.BlockSpec((1, ts, D_B + D_C), lambda bi, ti: (bi, ti, 0)),
        scratch_shapes=[pltpu.VMEM((ts + HALO, D_C), jnp.float32),
                        pltpu.VMEM((ts + HALO, D_B), jnp.float32),
                        pltpu.VMEM((SUBLANES - 1, ts + HALO - SUBLANES, D_C), jnp.float32)],
        compiler_params=_cparams(2),
        name="pool_conv",
    )(ubc, wpool_bd, pscale, convw, convb, lng, lnb)


KA = 512
ONES_ROWS = 16
FOLD_ROWS = 64
SEL_BASE_PASSES = 13


def _dsa_kernel(ik_ref, ckv_ref, ckvT_ref, qlatT_ref, iqT_ref, wT_ref, wuvT_ref,
                ya_ref, sc_scr, acc_scr, m_scr, *, topk):
    i = pl.program_id(1)
    qps = KA // Q_BLOCK
    nsub = (i + qps) // qps
    q_pos = i * Q_BLOCK + lax.broadcasted_iota(jnp.int32, (1, LANES), 1)
    n_causal = q_pos + 1
    target = jnp.minimum(topk, n_causal)

    def fold(v, op):
        return op(v.reshape(KA // FOLD_ROWS, FOLD_ROWS, LANES), axis=0)

    def sub_off(j):
        return pl.multiple_of(j * KA, KA)

    def key_pos(off):
        return off + lax.broadcasted_iota(jnp.int32, (KA, LANES), 0)

    def idx_group(js, masks, stats):
        offs = [sub_off(j) for j in js]
        lts = [jnp.dot(ik_ref[0, pl.ds(off, KA), :], iqT_ref[0], preferred_element_type=jnp.float32)
               for off in offs]
        for off, lt, masked in zip(offs, lts, masks):
            sc = jnp.zeros((KA, LANES), jnp.float32)
            for h in range(N_IDX_HEADS):
                sc = sc + jnp.maximum(lt[:, h * LANES:(h + 1) * LANES], 0.0) * wT_ref[0, h:h + 1, :]
            sc_for_min = sc
            if masked:
                causal = key_pos(off) <= q_pos
                sc_for_min = jnp.where(causal, sc, jnp.inf)
                sc = jnp.where(causal, sc, -jnp.inf)
            sc_scr[pl.ds(off, KA), :] = sc
            mx, mn, ge0, gt0 = stats
            stats = (jnp.maximum(mx, fold(sc, jnp.max)),
                     jnp.minimum(mn, fold(sc_for_min, jnp.min)),
                     ge0 + fold(jnp.where(sc >= 0.0, 1, 0).astype(jnp.int32), jnp.sum),
                     gt0 + fold(jnp.where(sc > 0.0, 1, 0).astype(jnp.int32), jnp.sum))
        return stats

    stats = (jnp.full((FOLD_ROWS, LANES), -jnp.inf, jnp.float32), jnp.full((FOLD_ROWS, LANES), jnp.inf, jnp.float32),
             jnp.zeros((FOLD_ROWS, LANES), jnp.int32), jnp.zeros((FOLD_ROWS, LANES), jnp.int32))
    n_before = nsub - 1
    stats = lax.fori_loop(0, n_before // 2,
                          lambda p, st: idx_group([2 * p, 2 * p + 1], [False, False], st), stats)
    stats = lax.cond(n_before % 2 == 1,
                     lambda st: idx_group([nsub - 2, nsub - 1], [False, True], st),
                     lambda st: idx_group([nsub - 1], [True], st), stats)
    rmax = jnp.max(stats[0], axis=0, keepdims=True)
    rmin = jnp.min(stats[1], axis=0, keepdims=True)
    c_ge0 = jnp.sum(stats[2], axis=0, keepdims=True)
    c_gt0 = jnp.sum(stats[3], axis=0, keepdims=True)

    def count_where(pred):
        def body(j, acc):
            off = sub_off(j)
            ind = jnp.where(pred(sc_scr[pl.ds(off, KA), :], off), 1, 0).astype(jnp.int32)
            return acc + fold(ind, jnp.sum)
        acc = lax.fori_loop(0, nsub, body, jnp.zeros((FOLD_ROWS, LANES), jnp.int32))
        return jnp.sum(acc, axis=0, keepdims=True)

    def unfinished(c_lo, fin):
        return (c_lo != target) & (fin == 0)

    def sel_pass(state):
        lo, hi, c_lo, fin = state
        mid = 0.5 * lo + 0.5 * hi
        mid = jnp.where(mid <= lo, hi, mid)
        active = unfinished(c_lo, fin)
        c = count_where(lambda blk, off: blk >= mid)
        ge = c >= target
        up = active & ge
        dn = active & jnp.logical_not(ge)
        fin = jnp.where(active & (mid == hi), 1, fin)
        return jnp.where(up, mid, lo), jnp.where(dn, mid, hi), jnp.where(up, c, c_lo), fin

    def any_lane(mask):
        return jnp.max(mask.astype(jnp.int32))

    at_zero = (c_gt0 < target) & (target <= c_ge0)
    above = c_gt0 >= target
    zero = jnp.zeros((1, LANES), jnp.float32)
    lo0 = jnp.where(at_zero | above, zero, rmin)
    hi0 = jnp.where(above, rmax, zero)
    c_lo0 = jnp.where(at_zero | above, c_ge0, n_causal)
    state0 = (lo0, hi0, c_lo0, at_zero.astype(jnp.int32))
    go0 = any_lane(unfinished(state0[2], state0[3]))

    n_blocks = i + 1
    n_fixed = SEL_BASE_PASSES + sum((n_blocks >= (1 << k)).astype(jnp.int32) for k in range(1, 16))
    state = lax.cond(go0 > 0,
                     lambda st: lax.fori_loop(0, n_fixed, lambda _, s_: sel_pass(s_), st),
                     lambda st: st, state0)

    def sel_more(carry):
        st = sel_pass(carry[0])
        return st, any_lane(unfinished(st[2], st[3]))

    state, _ = lax.while_loop(lambda carry: carry[1] > 0, sel_more,
                              (state, any_lane(unfinished(state[2], state[3]))))
    lo, _, c_lo, _ = state

    tie = c_lo > target

    @pl.when(any_lane(tie) > 0)
    def _():
        need = target - count_where(lambda blk, off: blk > lo)

        def j_body(_, st):
            j_lo, j_hi = st
            j_mid = (j_lo + j_hi) >> 1
            f = count_where(lambda blk, off: (blk == lo) & (key_pos(off) < j_mid))
            ok = f >= need
            return jnp.where(ok, j_lo, j_mid), jnp.where(ok, j_mid, j_hi)

        n_steps = int(sc_scr.shape[0]).bit_length()
        _, j_cut = lax.fori_loop(0, n_steps, j_body,
                                 (jnp.zeros((1, LANES), jnp.int32), jnp.full((1, LANES), nsub * KA, jnp.int32)))

        def drop_body(j, carry):
            off = sub_off(j)
            blk = sc_scr[pl.ds(off, KA), :]
            drop = tie & (blk == lo) & (key_pos(off) >= j_cut)
            sc_scr[pl.ds(off, KA), :] = jnp.where(drop, -jnp.inf, blk)
            return carry

        lax.fori_loop(0, nsub, drop_body, 0)

    hq = N_HEADS_A * LANES
    eye_rep = jnp.where(lax.broadcasted_iota(jnp.int32, (LANES, hq), 0)
                        == lax.broadcasted_iota(jnp.int32, (LANES, hq), 1) % LANES, 1.0, 0.0).astype(MXU_DTYPE)
    q_aug = jnp.concatenate([qlatT_ref[0], eye_rep], axis=0)
    ones_rows = jnp.ones((ONES_ROWS, KA), MXU_DTYPE)
    m_scr[...] = jnp.full(m_scr.shape, NEG_BIG, jnp.float32)
    acc_scr[...] = jnp.zeros(acc_scr.shape, jnp.float32)

    def logits(off):
        bias = jnp.where(sc_scr[pl.ds(off, KA), :] >= lo, 0.0, NEG_BIG).astype(MXU_DTYPE)
        return jnp.dot(jnp.concatenate([ckv_ref[0, pl.ds(off, KA), :], bias], axis=1), q_aug,
                       preferred_element_type=jnp.float32)

    def softmax_pv(off, st):
        ps = []
        alphas = []
        for h in range(N_HEADS_A):
            sl = slice(h * LANES, (h + 1) * LANES)
            s = st[:, sl]
            m_old = m_scr[:, sl]
            m_new = jnp.maximum(m_old, jnp.max(s, axis=0, keepdims=True))
            m_scr[:, sl] = m_new
            ps.append(jnp.exp2(s - m_new).astype(MXU_DTYPE))
            alphas.append(jnp.exp2(m_old - m_new))
        p_all = jnp.concatenate(ps, axis=1)
        alpha_all = jnp.concatenate(alphas, axis=1)
        v_aug = jnp.concatenate([ckvT_ref[:, pl.ds(off, KA)], ones_rows], axis=0)
        acc_scr[...] = acc_scr[...] * alpha_all + jnp.dot(v_aug, p_all, preferred_element_type=jnp.float32)

    def att_group(js):
        offs = [sub_off(j) for j in js]
        sts = [logits(off) for off in offs]
        for off, st in zip(offs, sts):
            softmax_pv(off, st)

    def att_pair(p, carry):
        att_group([2 * p, 2 * p + 1])
        return carry

    lax.fori_loop(0, nsub // 2, att_pair, 0)

    @pl.when(nsub % 2 == 1)
    def _():
        att_group([nsub - 1])

    denom = acc_scr[KV_RANK:KV_RANK + 1, :]
    o_t = (acc_scr[0:KV_RANK, :] * (1.0 / denom)).astype(MXU_DTYPE)
    ys = []
    for h in range(N_HEADS_A):
        ys.append(jnp.dot(wuvT_ref[h], o_t[:, h * LANES:(h + 1) * LANES],
                          preferred_element_type=jnp.float32))
    ya_ref[0] = jnp.concatenate(ys, axis=0).T.astype(ya_ref.dtype)


def _dsa_call(ik, ckv, ckvT, qlatT, iqT, wT, wuvT, b, s):
    nb = s // Q_BLOCK
    hq = N_HEADS_A * LANES
    topk = min(TOPK_MAX, s // 4)
    return pl.pallas_call(
        functools.partial(_dsa_kernel, topk=topk),
        out_shape=jax.ShapeDtypeStruct((b, s, D_A), MXU_DTYPE),
        grid=(b, nb),
        in_specs=[
            pl.BlockSpec((1, s, LANES), lambda bi, qi: (bi, 0, 0)),
            pl.BlockSpec((1, s, KV_RANK), lambda bi, qi: (bi, 0, 0)),
            pl.BlockSpec((KV_RANK, s), lambda bi, qi: (0, bi)),
            pl.BlockSpec((1, KV_RANK, hq), lambda bi, qi: (bi * nb + qi, 0, 0)),
            pl.BlockSpec((1, LANES, hq), lambda bi, qi: (bi * nb + qi, 0, 0)),
            pl.BlockSpec((1, N_IDX_HEADS, LANES), lambda bi, qi: (bi * nb + qi, 0, 0)),
            pl.BlockSpec((N_HEADS_A, HEAD_DIM_A, KV_RANK), lambda bi, qi: (0, 0, 0)),
        ],
        out_specs=pl.BlockSpec((1, Q_BLOCK, D_A), lambda bi, qi: (bi, qi, 0)),
        scratch_shapes=[
            pltpu.VMEM((s, LANES), jnp.float32),
            pltpu.VMEM((KV_RANK + ONES_ROWS, hq), jnp.float32),
            pltpu.VMEM((1, hq), jnp.float32),
        ],
        compiler_params=_cparams(2),
        name="dsa_attention",
    )(ik, ckv, ckvT, qlatT, iqT, wT, wuvT)


def _out_kernel(x_ref, ya_ref, ybc_ref, wo_ref, g_ref, b_ref, rwT_ref, rbias_ref,
                x1_ref, gatesT_ref, sel_scr, aff_scr):
    mix = jnp.dot(ya_ref[...], wo_ref[0:D_A, :], preferred_element_type=jnp.float32)
    mix = mix + jnp.dot(ybc_ref[...], wo_ref[D_A:, :], preferred_element_type=jnp.float32)
    x1 = _layer_norm(ALPHA * x_ref[...] + mix, g_ref[...], b_ref[...])
    x1_ref[...] = x1

    def split(v):
        hi = v.astype(MXU_DTYPE)
        return hi, (v - hi.astype(jnp.float32)).astype(MXU_DTYPE)

    def dot_nt(a, b):
        return lax.dot_general(a, b, (((1,), (1,)), ((), ())), preferred_element_type=jnp.float32)

    r_hi, r_lo = split(rwT_ref[...])
    x_hi, x_lo = split(x1)
    logits_t = dot_nt(r_hi, x_hi) + (dot_nt(r_hi, x_lo) + dot_nt(r_lo, x_hi))
    aff = _sigmoid(logits_t)
    aff_scr[...] = aff
    sel_scr[...] = aff + rbias_ref[...]

    def row(ref, e):
        return ref[e:e + 1, :]

    grp_scores = []
    for g in range(N_EXPERT_GROUPS):
        r = [row(sel_scr, g * EXPERTS_PER_GROUP + k) for k in range(EXPERTS_PER_GROUP)]
        m1, n1 = jnp.maximum(r[0], r[1]), jnp.minimum(r[0], r[1])
        m2, n2 = jnp.maximum(r[2], r[3]), jnp.minimum(r[2], r[3])
        grp_scores.append(jnp.maximum(m1, m2) + jnp.maximum(jnp.minimum(m1, m2), jnp.maximum(n1, n2)))
    best = jnp.zeros_like(grp_scores[0], dtype=jnp.int32)
    cur = grp_scores[0]
    for g in range(1, N_EXPERT_GROUPS):
        better = grp_scores[g] > cur
        best = jnp.where(better, g, best)
        cur = jnp.where(better, grp_scores[g], cur)

    sel_aff = []
    for e in range(N_EXPERTS):
        g = e // EXPERTS_PER_GROUP
        se = row(sel_scr, e)
        rank = jnp.zeros_like(best)
        for f in range(g * EXPERTS_PER_GROUP, (g + 1) * EXPERTS_PER_GROUP):
            if f == e:
                continue
            sf = row(sel_scr, f)
            ahead = (sf > se) | ((sf == se) & (f < e))
            rank = rank + ahead.astype(jnp.int32)
        chosen = (best == g) & (rank < 2)
        sel_aff.append(jnp.where(chosen, row(aff_scr, e), 0.0))
    denom = sel_aff[0]
    for e in range(1, N_EXPERTS):
        denom = denom + sel_aff[e]
    inv = 1.0 / denom
    for e in range(N_EXPERTS):
        gatesT_ref[e:e + 1, :] = sel_aff[e] * inv


def _out_call(xf, ya, ybc, wo, g, b, rwT, rbias, tm):
    n, d = xf.shape
    full = lambda i: (0, 0)
    return pl.pallas_call(
        _out_kernel,
        out_shape=(jax.ShapeDtypeStruct((n, d), jnp.float32),
                   jax.ShapeDtypeStruct((N_EXPERTS, n), jnp.float32)),
        grid=(n // tm,),
        in_specs=[
            pl.BlockSpec((tm, d), lambda i: (i, 0)),
            pl.BlockSpec((tm, D_A), lambda i: (i, 0)),
            pl.BlockSpec((tm, D_B + D_C), lambda i: (i, 0)),
            pl.BlockSpec((d, d), full),
            pl.BlockSpec((1, d), full),
            pl.BlockSpec((1, d), full),
            pl.BlockSpec((N_EXPERTS, d), full),
            pl.BlockSpec((N_EXPERTS, 1), full),
        ],
        out_specs=(pl.BlockSpec((tm, d), lambda i: (i, 0)),
                   pl.BlockSpec((N_EXPERTS, tm), lambda i: (0, i))),
        scratch_shapes=[pltpu.VMEM((N_EXPERTS, tm), jnp.float32),
                        pltpu.VMEM((N_EXPERTS, tm), jnp.float32)],
        compiler_params=_cparams(1),
        name="out_proj_router",
    )(xf, ya, ybc, wo, g, b, rwT, rbias)


MOE_SUB = 256
MOE_CAP = 48


def _moe_kernel(x1_ref, gates_t_ref, tri_ref, wg_ref, wu_ref, wd_ref, g_ref, b_ref,
                out_ref, xb_scr, acc_scr, rank_scr, oh_scr, y_scr, xe_scr, gate_scr):
    e = pl.program_id(1)
    tm = x1_ref.shape[0]
    n_sub = tm // MOE_SUB
    n_exp = gates_t_ref.shape[0]

    def sub(j):
        return slice(j * MOE_SUB, (j + 1) * MOE_SUB)

    def slots(g_row, rank_row):
        return jnp.where(g_row != 0.0, rank_row.astype(jnp.int32), -1)

    def one_hot_rows(b, slot, g_row, j):
        row = b * MOE_CAP + lax.broadcasted_iota(jnp.int32, (MOE_CAP, MOE_SUB), 0)
        hit = row == slot[:, sub(j)]
        onehot = jnp.where(hit, 1.0, 0.0).astype(MXU_DTYPE)
        gate = jnp.sum(jnp.where(hit, g_row[:, sub(j)], 0.0), axis=1, keepdims=True)
        return onehot, gate

    @pl.when(e == 0)
    def _():
        xb_scr[...] = x1_ref[...].astype(xb_scr.dtype)
        acc_scr[...] = jnp.zeros(acc_scr.shape, jnp.float32)
        routed_all = jnp.where(gates_t_ref[...] != 0.0, 1.0, 0.0).astype(MXU_DTYPE)
        rank = jnp.dot(routed_all, tri_ref[...], preferred_element_type=jnp.float32)
        rank_scr[...] = rank
        for ee in range(n_exp):
            g_r = gates_t_ref[ee:ee + 1, :]
            slot_e = slots(g_r, rank[ee:ee + 1, :])
            for j in range(n_sub):
                onehot, gate = one_hot_rows(0, slot_e, g_r, j)
                oh_scr[j, ee * MOE_CAP:(ee + 1) * MOE_CAP, :] = onehot
                gate_scr[j, ee * MOE_CAP:(ee + 1) * MOE_CAP, :] = gate
        for j in range(n_sub):
            xe_scr[j] = jnp.dot(oh_scr[j], xb_scr[sub(j), :],
                                preferred_element_type=jnp.float32).astype(xe_scr.dtype)

    def scatter(onehot, y):
        return lax.dot_general(onehot, y, (((0,), (0,)), ((), ())), preferred_element_type=jnp.float32)

    def ffn(xe, gate):
        hg = jnp.dot(xe, wg_ref[0, 0], preferred_element_type=jnp.float32)
        hu = jnp.dot(xe, wu_ref[0, 0], preferred_element_type=jnp.float32)
        h = (hg * _sigmoid(hg)) * hu * gate
        return jnp.dot(h.astype(MXU_DTYPE), wd_ref[0, 0], preferred_element_type=jnp.float32).astype(MXU_DTYPE)

    base = pl.multiple_of(e * MOE_CAP, MOE_CAP)
    y0 = ffn(jnp.concatenate([xe_scr[j, pl.ds(base, MOE_CAP), :] for j in range(n_sub)], axis=0),
             jnp.concatenate([gate_scr[j, pl.ds(base, MOE_CAP), :] for j in range(n_sub)], axis=0))
    for j in range(n_sub):
        y_scr[j, pl.ds(base, MOE_CAP), :] = y0[j * MOE_CAP:(j + 1) * MOE_CAP]

    g_row = gates_t_ref[pl.ds(e, 1), :]
    slot = slots(g_row, rank_scr[pl.ds(e, 1), :])
    n_pass = (jnp.max(slot) + MOE_CAP) // MOE_CAP

    def expert_rows(b):
        pieces = [one_hot_rows(b, slot, g_row, j) for j in range(n_sub)]
        onehots = [p[0] for p in pieces]
        xe = jnp.concatenate([jnp.dot(onehots[j], xb_scr[sub(j), :],
                                      preferred_element_type=jnp.float32).astype(MXU_DTYPE)
                              for j in range(n_sub)], axis=0)
        y = ffn(xe, jnp.concatenate([p[1] for p in pieces], axis=0))
        return onehots, [y[j * MOE_CAP:(j + 1) * MOE_CAP] for j in range(n_sub)]

    def overflow_body(b, carry):
        onehots, ys = expert_rows(b)
        for j in range(n_sub):
            acc_scr[sub(j), :] += scatter(onehots[j], ys[j])
        return carry

    lax.fori_loop(1, n_pass, overflow_body, 0)

    @pl.when(e == pl.num_programs(1) - 1)
    def _():
        for j in range(n_sub):
            moe_out = acc_scr[sub(j), :] + scatter(oh_scr[j], y_scr[j])
            out_ref[sub(j), :] = _layer_norm(ALPHA * x1_ref[sub(j), :] + moe_out, g_ref[...], b_ref[...])


def _moe_call(x1, gates_t, wg, wu, wd, layer, g, b, tm):
    n, d = x1.shape
    _, ne, _, de = wg.shape
    assert tm % MOE_SUB == 0
    t_idx = jnp.arange(tm)
    tri = ((t_idx[:, None] < t_idx[None, :]) & (t_idx[:, None] // MOE_SUB == t_idx[None, :] // MOE_SUB)
           ).astype(MXU_DTYPE)
    n_sub = tm // MOE_SUB
    return pl.pallas_call(
        _moe_kernel,
        out_shape=jax.ShapeDtypeStruct((n, d), jnp.float32),
        grid=(n // tm, ne),
        in_specs=[
            pl.BlockSpec((tm, d), lambda i, e: (i, 0)),
            pl.BlockSpec((ne, tm), lambda i, e: (0, i)),
            pl.BlockSpec((tm, tm), lambda i, e: (0, 0)),
            pl.BlockSpec((1, 1, d, de), lambda i, e: (layer, e, 0, 0)),
            pl.BlockSpec((1, 1, d, de), lambda i, e: (layer, e, 0, 0)),
            pl.BlockSpec((1, 1, de, d), lambda i, e: (layer, e, 0, 0)),
            pl.BlockSpec((1, d), lambda i, e: (0, 0)),
            pl.BlockSpec((1, d), lambda i, e: (0, 0)),
        ],
        out_specs=pl.BlockSpec((tm, d), lambda i, e: (i, 0)),
        scratch_shapes=[pltpu.VMEM((tm, d), MXU_DTYPE), pltpu.VMEM((tm, d), jnp.float32),
                        pltpu.VMEM((ne, tm), jnp.float32),
                        pltpu.VMEM((n_sub, ne * MOE_CAP, MOE_SUB), MXU_DTYPE),
                        pltpu.VMEM((n_sub, ne * MOE_CAP, d), MXU_DTYPE),
                        pltpu.VMEM((n_sub, ne * MOE_CAP, d), MXU_DTYPE),
                        pltpu.VMEM((n_sub, ne * MOE_CAP, 1), jnp.float32)],
        compiler_params=_cparams(2),
        name="moe_ffn",
    )(x1, gates_t, tri, wg, wu, wd, g, b)


def _arrange_w_in(w):
    d = w.shape[0]
    z = lambda k: jnp.zeros((d, k), w.dtype)
    idx_block = jnp.concatenate([w[:, _R_IK:_R_IW], z(IW_LANE - IDX_DIM), w[:, _R_IW:_R_POOL],
                                 z(LANES - IW_LANE - N_IDX_HEADS)], axis=1)
    return jnp.concatenate([w[:, _R_Q:_R_KV], w[:, _R_KV:_R_IQ], idx_block, w[:, _R_IQ:_R_IK],
                            w[:, _R_CONV:_R_END], w[:, _R_POOL:_R_CONV]], axis=1)


def _block_diag(w_pool):
    g, c, _ = w_pool.shape
    out = jnp.zeros((g * c, g * c), w_pool.dtype)
    for gi in range(g):
        out = out.at[gi * c:(gi + 1) * c, gi * c:(gi + 1) * c].set(w_pool[gi])
    return out


def _pick(n, pref):
    t = pref
    while n % t:
        t //= 2
    return t


def kernel(x, w_in, kv_norm_g, w_uk, w_uv, w_pool, pool_scale, conv_w, conv_b, conv_ln_g,
           conv_ln_b, w_o, ln1_g, ln1_b, router_w, router_bias, w_gate, w_up, w_down,
           ln2_g, ln2_b):
    b, s, d = x.shape
    n = b * s
    depth = w_in.shape[0]
    assert s % KA == 0 and d == 1024
    xf = x.reshape(n, d)
    rwT = router_w.T
    rbias = router_bias.reshape(N_EXPERTS, 1)
    tm_in = _pick(n, 512)
    ts_bc = _pick(s, 512)
    tm_out = _pick(n, 512)
    tm_moe = _pick(n, 1024)
    wg_all, wu_all, wd_all = (w.astype(MXU_DTYPE) for w in (w_gate, w_up, w_down))
    for l in range(depth):
        w_arr = _arrange_w_in(w_in[l]).astype(MXU_DTYPE)
        ubc, ckv, ckvT, ik, qlatT, iqT, wT = _in_call(
            xf, w_arr, kv_norm_g[l].reshape(1, KV_RANK), w_uk[l].astype(MXU_DTYPE), tm_in)
        convw = jnp.concatenate([conv_w[l].reshape(CONV_WIDTH, D_C),
                                 jnp.zeros((HALO - CONV_WIDTH, D_C), conv_w.dtype)], axis=0)
        ybc = _bc_call(ubc.reshape(b, s, C_END - C_UC), _block_diag(w_pool[l]).astype(MXU_DTYPE),
                       pool_scale[l].reshape(1, D_B), convw, conv_b[l].reshape(1, D_C),
                       conv_ln_g[l].reshape(1, D_C), conv_ln_b[l].reshape(1, D_C), ts_bc)
        wuvT = jnp.swapaxes(w_uv[l], 1, 2).astype(MXU_DTYPE)
        ya = _dsa_call(ik.reshape(b, s, LANES), ckv.reshape(b, s, KV_RANK), ckvT, qlatT, iqT, wT, wuvT, b, s)
        x1, gates_t = _out_call(xf, ya.reshape(n, D_A), ybc.reshape(n, D_B + D_C), w_o[l].astype(MXU_DTYPE),
                                ln1_g[l].reshape(1, d), ln1_b[l].reshape(1, d), rwT, rbias, tm_out)
        xf = _moe_call(x1, gates_t, wg_all, wu_all, wd_all, l,
                       ln2_g[l].reshape(1, d), ln2_b[l].reshape(1, d), tm_moe)
    return xf.reshape(b, s, d)
```
